```python
import math
import jax
import jax.numpy as jnp
from jax import lax
import numpy as np

D_MODEL = 1024
BATCH = 16
SEQ = 2048
DEPTH = 4
DEC_BATCH = 128
DEC_SEQ = 4
PAST_LEN = 8192
PAGE_SIZE = 128

N_EVEN = (DEPTH + 1) // 2
N_ODD = DEPTH // 2
EPS = 1e-6
ROPE_THETA = 500000.0
Q_BLOCK = 128
D_FF = 2816

H_A = 8
HD_A = 64
KV_A = 2
ROT_A = HD_A // 4
H_I = 8
D_I = 32
ROT_I = D_I // 4
TOPK_MAX = 256

H_B = 8
NOPE_B = 64
ROPE_B = 32
V_B = 64
Q_LORA = 256
KV_LORA = 128
MLA_SCALE = (NOPE_B + ROPE_B) ** -0.5

H_C = 8
DK_C = 64
DV_C = 64
CONV_W = 4
CHUNK_C = 64
C_CONV_CH = H_C * (2 * DK_C + DV_C)

H_D = 8
HD_D = 64
KV_D = 4

EVEN_SPLITS = (H_A * HD_A, KV_A * HD_A, KV_A * HD_A, H_I * D_I, D_I, H_I, Q_LORA, KV_LORA, ROPE_B)
ODD_SPLITS = (H_C * DK_C, H_C * DK_C, H_C * DV_C, H_C, H_C, H_C * DV_C, H_D * HD_D, KV_D * HD_D, KV_D * HD_D)
EVEN_IN = sum(EVEN_SPLITS)
ODD_IN = sum(ODD_SPLITS)
MIX_OUT_EVEN = H_A * HD_A + H_B * V_B
MIX_OUT_ODD = H_C * DV_C + H_D * HD_D
STATE_NAMES = ('a_k', 'a_v', 'a_idx', 'b_ckv', 'b_krope', 'd_k', 'd_v', 'c_ssm', 'c_conv')

kernel_name = 'hybrid_dsa_mla_gdn_stickbreak_step'


def _rmsnorm(x, g):
    xf = x.astype(jnp.float32)
    y = xf * lax.rsqrt(jnp.mean(xf * xf, axis=-1, keepdims=True) + EPS)
    return (y * g.astype(jnp.float32)).astype(x.dtype)


def _l2norm(x):
    return x * lax.rsqrt(jnp.sum(x * x, axis=-1, keepdims=True) + EPS)


def _rope(x, pos, rot):
    half = rot // 2
    inv = ROPE_THETA ** (-jnp.arange(half, dtype=jnp.float32) / half)
    ang = pos.astype(jnp.float32)[:, None] * inv[None, :]
    cos = jnp.cos(ang)[:, None, :]
    sin = jnp.sin(ang)[:, None, :]
    xf = x.astype(jnp.float32)
    x1, x2 = xf[..., :half], xf[..., half:rot]
    out = jnp.concatenate([x1 * cos - x2 * sin, x2 * cos + x1 * sin, xf[..., rot:]], axis=-1)
    return out.astype(x.dtype)


def _split(z, sizes):
    return jnp.split(z, [int(o) for o in np.cumsum(sizes)[:-1]], axis=-1)


def _swiglu(x, wg, wu, wd):
    return (jax.nn.silu(x @ wg) * (x @ wu)) @ wd


def _blocked(block_fn, n):
    out = lax.map(block_fn, jnp.arange(n // Q_BLOCK) * Q_BLOCK)
    return jnp.swapaxes(out, 0, 1).reshape(out.shape[1], n, out.shape[-1])


def _gather_pages(pool, j, page_table):
    rows = pool[j, page_table]
    return rows.reshape((page_table.shape[0], -1) + pool.shape[3:])


def _gather_rows(pool, j, page_table, new, idx):
    past = page_table.shape[1] * PAGE_SIZE
    bi = jnp.arange(idx.shape[0])[:, None, None]
    ip = jnp.minimum(idx, past - 1)
    rows_past = pool[j, page_table[bi, ip // PAGE_SIZE], ip % PAGE_SIZE]
    rows_new = new[bi, jnp.clip(idx - past, 0, new.shape[1] - 1)]
    from_past = (idx < past).reshape(idx.shape + (1,) * (new.ndim - 2))
    return jnp.where(from_past, rows_past, rows_new)


def _indexer_scores(qi, wi, ki, mask):
    rel = jax.nn.relu(jnp.einsum('bthd,bsd->bths', qi, ki).astype(jnp.float32))
    score = jnp.einsum('bths,bth->bts', rel, wi.astype(jnp.float32))
    return jnp.where(mask[None], score, -jnp.inf)


def _sparse_attend(q, kg, vg, valid):
    B, T = q.shape[:2]
    qg = q.reshape(B, T, KV_A, H_A // KV_A, HD_A)
    s = jnp.einsum('btkgd,btnkd->btkgn', qg, kg).astype(jnp.float32) * HD_A ** -0.5
    s = jnp.where(valid[:, :, None, None, :], s, -jnp.inf)
    p = jax.nn.softmax(s, axis=-1).astype(vg.dtype)
    o = jnp.einsum('btkgn,btnkd->btkgd', p, vg)
    return o.reshape(B, T, H_A * HD_A)


def _dsa_prompt(q, k, v, qi, wi, ki):
    B, S = q.shape[:2]
    n_top = min(TOPK_MAX, S // 4)
    bi = jnp.arange(B)[:, None, None]
    keys = jnp.arange(S)

    def block(t0):
        qb = lax.dynamic_slice_in_dim(q, t0, Q_BLOCK, 1)
        qib = lax.dynamic_slice_in_dim(qi, t0, Q_BLOCK, 1)
        wib = lax.dynamic_slice_in_dim(wi, t0, Q_BLOCK, 1)
        mask = keys[None, :] <= (t0 + jnp.arange(Q_BLOCK))[:, None]
        top, idx = lax.top_k(_indexer_scores(qib, wib, ki, mask), n_top)
        return _sparse_attend(qb, k[bi, idx], v[bi, idx], jnp.isfinite(top))

    return _blocked(block, S)


def _dsa_sample(q, k, v, qi, wi, ki, pool_k, pool_v, pool_i, j, page_table):
    T = q.shape[1]
    past = page_table.shape[1] * PAGE_SIZE
    n_keys = past + T
    n_top = min(TOPK_MAX, n_keys // 4)
    ki_all = jnp.concatenate([_gather_pages(pool_i, j, page_table), ki], axis=1)
    mask = jnp.arange(n_keys)[None, :] <= (past + jnp.arange(T))[:, None]
    top, idx = lax.top_k(_indexer_scores(qi, wi, ki_all, mask), n_top)
    kg = _gather_rows(pool_k, j, page_table, k, idx)
    vg = _gather_rows(pool_v, j, page_table, v, idx)
    return _sparse_attend(q, kg, vg, jnp.isfinite(top))


def _mla_prompt(q_nope, q_rope, c, k_r, w_kvb):
    B, S = c.shape[:2]
    kv = (c @ w_kvb).reshape(B, S, H_B, NOPE_B + V_B)
    k_nope, v = kv[..., :NOPE_B], kv[..., NOPE_B:]
    keys = jnp.arange(S)

    def block(t0):
        qn = lax.dynamic_slice_in_dim(q_nope, t0, Q_BLOCK, 1)
        qr = lax.dynamic_slice_in_dim(q_rope, t0, Q_BLOCK, 1)
        s = (jnp.einsum('bthd,bshd->bhts', qn, k_nope)
             + jnp.einsum('bthd,bsd->bhts', qr, k_r)).astype(jnp.float32) * MLA_SCALE
        mask = keys[None, :] <= (t0 + jnp.arange(Q_BLOCK))[:, None]
        p = jax.nn.softmax(jnp.where(mask, s, -jnp.inf), axis=-1).astype(v.dtype)
        return jnp.einsum('bhts,bshd->bthd', p, v).reshape(B, Q_BLOCK, H_B * V_B)

    return _blocked(block, S)


def _mla_sample(q_nope, q_rope, c, k_r, w_kvb, pool_c, pool_r, j, page_table):
    DB, T = c.shape[:2]
    past = page_table.shape[1] * PAGE_SIZE
    c_all = jnp.concatenate([_gather_pages(pool_c, j, page_table), c], axis=1)
    r_all = jnp.concatenate([_gather_pages(pool_r, j, page_table), k_r], axis=1)
    w = w_kvb.reshape(KV_LORA, H_B, NOPE_B + V_B)
    q_lat = jnp.einsum('bthd,chd->bthc', q_nope, w[..., :NOPE_B])
    s = (jnp.einsum('bthc,bsc->bhts', q_lat, c_all)
         + jnp.einsum('bthd,bsd->bhts', q_rope, r_all)).astype(jnp.float32) * MLA_SCALE
    mask = jnp.arange(past + T)[None, :] <= (past + jnp.arange(T))[:, None]
    p = jax.nn.softmax(jnp.where(mask, s, -jnp.inf), axis=-1).astype(c_all.dtype)
    o_lat = jnp.einsum('bhts,bsc->bthc', p, c_all)
    return jnp.einsum('bthc,chd->bthd', o_lat, w[..., NOPE_B:]).reshape(DB, T, H_B * V_B)


def _causal_conv(xc, buf, w):
    T = xc.shape[1]
    full = jnp.concatenate([buf, xc], axis=1)
    y = sum(full[:, i:i + T] * w[i] for i in range(CONV_W))
    return jax.nn.silu(y), full[:, T:]


def _gated_delta(q, k, v, g, beta, s0):
    B, T, H, dk = q.shape
    dv = v.shape[-1]
    C = math.gcd(T, CHUNK_C)
    N = T // C
    f32 = jnp.float32
    q = _l2norm(q.astype(f32)) * dk ** -0.5
    k = _l2norm(k.astype(f32))
    v = v.astype(f32)

    def chunks(a):
        a = a.reshape((B, N, C, H) + a.shape[3:])
        return jnp.moveaxis(a, (1, 3), (0, 2))

    qc, kc, vc = chunks(q), chunks(k), chunks(v)
    gc = jnp.cumsum(chunks(g.astype(f32)), axis=-1)
    bc = chunks(beta.astype(f32))
    incl = jnp.tril(jnp.ones((C, C), bool))
    strict = jnp.tril(jnp.ones((C, C), bool), -1)
    gam = jnp.exp(jnp.where(incl, gc[..., :, None] - gc[..., None, :], -jnp.inf))
    kb = kc * bc[..., None]
    a_kk = jnp.where(strict, jnp.einsum('nbhid,nbhjd->nbhij', kb, kc) * gam, 0.0)
    m = a_kk + jnp.eye(C, dtype=f32)
    u = lax.linalg.triangular_solve(m, vc * bc[..., None], left_side=True, lower=True, unit_diagonal=True)
    w = lax.linalg.triangular_solve(m, kb * jnp.exp(gc)[..., None], left_side=True, lower=True, unit_diagonal=True)
    a_qk = jnp.einsum('nbhid,nbhjd->nbhij', qc, kc) * gam

    def step(s, xs):
        q_, k_, u_, w_, g_, a_ = xs
        v_new = u_ - jnp.einsum('bhcd,bhde->bhce', w_, s)
        o = (jnp.einsum('bhcd,bhde->bhce', q_ * jnp.exp(g_)[..., None], s)
             + jnp.einsum('bhij,bhje->bhie', a_, v_new))
        g_last = g_[..., -1]
        k_dec = k_ * jnp.exp(g_last[..., None] - g_)[..., None]
        s = s * jnp.exp(g_last)[..., None, None] + jnp.einsum('bhcd,bhce->bhde', k_dec, v_new)
        return s, o

    s, o = lax.scan(step, s0.astype(f32), (qc, kc, u, w, gc, a_qk))
    o = jnp.moveaxis(o, (0, 2), (1, 3)).reshape(B, T, H, dv)
    return o, s


def _stick_breaking(q, k, v, qpos):
    B, T = q.shape[:2]
    S = k.shape[1]
    qg = q.reshape(B, T, KV_D, H_D // KV_D, HD_D)
    z = jnp.einsum('btkgd,bskd->bkgts', qg, k).astype(jnp.float32) * HD_D ** -0.5
    mask = jnp.arange(S)[None, :] < qpos[:, None]
    log_keep = jnp.where(mask, jax.nn.log_sigmoid(-z), 0.0)
    log_rest = lax.cumsum(log_keep, axis=z.ndim - 1, reverse=True) - log_keep
    a = jnp.where(mask, jnp.exp(jax.nn.log_sigmoid(z) + log_rest), 0.0)
    o = jnp.einsum('bkgts,bskd->btkgd', a.astype(v.dtype), v)
    return o.reshape(B, T, H_D * HD_D)


def _sb_prompt(q, k, v):
    def block(t0):
        qb = lax.dynamic_slice_in_dim(q, t0, Q_BLOCK, 1)
        return _stick_breaking(qb, k, v, t0 + jnp.arange(Q_BLOCK))

    return _blocked(block, q.shape[1])


def _even_mix(h, pos, w, j, cache):
    B, T, _ = h.shape
    z = h @ w['w_in_even'][j]
    qa, ka, va, qi, ki, wi, ql, ckv, krb = _split(z, EVEN_SPLITS)
    q_a = _rope(qa.reshape(B, T, H_A, HD_A), pos, ROT_A)
    k_a = _rope(ka.reshape(B, T, KV_A, HD_A), pos, ROT_A)
    v_a = va.reshape(B, T, KV_A, HD_A)
    q_i = _rope(qi.reshape(B, T, H_I, D_I), pos, ROT_I)
    k_i = _rope(ki[:, :, None, :], pos, ROT_I)[:, :, 0]
    w_i = wi * (H_I * D_I) ** -0.5
    qb = (_rmsnorm(ql, w['mla_q_norm'][j]) @ w['mla_w_qb'][j]).reshape(B, T, H_B, NOPE_B + ROPE_B)
    q_nope = qb[..., :NOPE_B]
    q_rope = _rope(qb[..., NOPE_B:], pos, ROPE_B)
    c = _rmsnorm(ckv, w['mla_kv_norm'][j])
    k_r = _rope(krb[:, :, None, :], pos, ROPE_B)[:, :, 0]
    w_kvb = w['mla_w_kvb'][j]
    if cache is None:
        o_a = _dsa_prompt(q_a, k_a, v_a, q_i, w_i, k_i)
        o_b = _mla_prompt(q_nope, q_rope, c, k_r, w_kvb)
    else:
        pt = cache['page_table']
        o_a = _dsa_sample(q_a, k_a, v_a, q_i, w_i, k_i, cache['cache_a_k'], cache['cache_a_v'],
                          cache['cache_a_idx'], j, pt)
        o_b = _mla_sample(q_nope, q_rope, c, k_r, w_kvb, cache['cache_b_ckv'], cache['cache_b_krope'], j, pt)
    y = jnp.concatenate([o_a, o_b], axis=-1) @ w['w_out_even'][j]
    return y, {'a_k': k_a, 'a_v': v_a, 'a_idx': k_i, 'b_ckv': c, 'b_krope': k_r}


def _odd_mix(h, pos, w, j, cache):
    B, T, _ = h.shape
    f32 = jnp.float32
    z = h @ w['w_in_odd'][j]
    qc, kc, vc, ac, bc, gate, qd, kd, vd = _split(z, ODD_SPLITS)
    if cache is None:
        conv_buf = jnp.zeros((B, CONV_W - 1, C_CONV_CH), h.dtype)
        s0 = jnp.zeros((B, H_C, DK_C, DV_C), f32)
    else:
        conv_buf = cache['state_c_conv'][j]
        s0 = cache['state_c_ssm'][j]
    xc, new_buf = _causal_conv(jnp.concatenate([qc, kc, vc], axis=-1), conv_buf, w['gdn_conv_w'][j])
    q_c, k_c, v_c = _split(xc, (H_C * DK_C, H_C * DK_C, H_C * DV_C))
    g = -jnp.exp(w['gdn_a_log'][j].astype(f32)) * jax.nn.softplus(ac.astype(f32) + w['gdn_dt_bias'][j].astype(f32))
    beta = jax.nn.sigmoid(bc.astype(f32))
    o_c, s_new = _gated_delta(q_c.reshape(B, T, H_C, DK_C), k_c.reshape(B, T, H_C, DK_C),
                              v_c.reshape(B, T, H_C, DV_C), g, beta, s0)
    o_c = (_rmsnorm(o_c, w['gdn_o_norm'][j])
           * jax.nn.silu(gate.reshape(B, T, H_C, DV_C).astype(f32))).astype(h.dtype).reshape(B, T, H_C * DV_C)
    q_d = qd.reshape(B, T, H_D, HD_D)
    k_d = kd.reshape(B, T, KV_D, HD_D)
    v_d = vd.reshape(B, T, KV_D, HD_D)
    if cache is None:
        o_d = _sb_prompt(q_d, k_d, v_d)
    else:
        pt = cache['page_table']
        k_all = jnp.concatenate([_gather_pages(cache['cache_d_k'], j, pt), k_d], axis=1)
        v_all = jnp.concatenate([_gather_pages(cache['cache_d_v'], j, pt), v_d], axis=1)
        o_d = _stick_breaking(q_d, k_all, v_all, pos)
    y = jnp.concatenate([o_c, o_d], axis=-1) @ w['w_out_odd'][j]
    return y, {'d_k': k_d, 'd_v': v_d, 'c_ssm': s_new.astype(h.dtype), 'c_conv': new_buf}


def _layer_stack(x, pos, w, cache):
    outs = {name: [] for name in STATE_NAMES}
    for l in range(DEPTH):
        j = l // 2
        x = x + 0.5 * _swiglu(_rmsnorm(x, w['norm_ffn'][l, 0]), w['w_ffn_gate'][l, 0],
                              w['w_ffn_up'][l, 0], w['w_ffn_down'][l, 0])
        hn = _rmsnorm(x, w['norm_mix'][l])
        if l % 2 == 0:
            y, st = _even_mix(hn, pos, w, j, cache)
        else:
            y, st = _odd_mix(hn, pos, w, j, cache)
        for name, val in st.items():
            outs[name].append(val)
        x = x + y
        x = x + 0.5 * _swiglu(_rmsnorm(x, w['norm_ffn'][l, 1]), w['w_ffn_gate'][l, 1],
                              w['w_ffn_up'][l, 1], w['w_ffn_down'][l, 1])
    return _rmsnorm(x, w['norm_final']), {name: jnp.stack(v) for name, v in outs.items()}


def setup_inputs(seed: int = 0) -> dict:
    key = jax.random.key(seed)
    ks = iter(jax.random.split(key, 48))
    f32 = jnp.float32

    def nrm(shape, scale=1.0):
        return jax.random.normal(next(ks), shape, f32) * scale

    def gain(shape):
        return 1.0 + nrm(shape, 0.02)

    n_pages = PAST_LEN // PAGE_SIZE
    n_used = DEC_BATCH * n_pages
    n_pool = n_used + n_used // 4
    page_table = jax.random.permutation(next(ks), n_pool)[:n_used].reshape(DEC_BATCH, n_pages).astype(jnp.int32)
    dt = jnp.exp(jax.random.uniform(next(ks), (N_ODD, H_C), f32, math.log(1e-3), math.log(1e-1)))
    a_log = jnp.log(jax.random.uniform(next(ks), (N_ODD, H_C), f32, 1.0, 16.0))
    return {
        'x_prompt': nrm((BATCH, SEQ, D_MODEL)),
        'x_sample': nrm((DEC_BATCH, DEC_SEQ, D_MODEL)),
        'cache_a_k': nrm((N_EVEN, n_pool, PAGE_SIZE, KV_A, HD_A)),
        'cache_a_v': nrm((N_EVEN, n_pool, PAGE_SIZE, KV_A, HD_A)),
        'cache_a_idx': nrm((N_EVEN, n_pool, PAGE_SIZE, D_I)),
        'cache_b_ckv': nrm((N_EVEN, n_pool, PAGE_SIZE, KV_LORA)),
        'cache_b_krope': nrm((N_EVEN, n_pool, PAGE_SIZE, ROPE_B)),
        'cache_d_k': nrm((N_ODD, n_pool, PAGE_SIZE, KV_D, HD_D)),
        'cache_d_v': nrm((N_ODD, n_pool, PAGE_SIZE, KV_D, HD_D)),
        'state_c_ssm': nrm((N_ODD, DEC_BATCH, H_C, DK_C, DV_C), 0.1),
        'state_c_conv': nrm((N_ODD, DEC_BATCH, CONV_W - 1, C_CONV_CH)),
        'page_table': page_table,
        'norm_ffn': gain((DEPTH, 2, D_MODEL)),
        'w_ffn_gate': nrm((DEPTH, 2, D_MODEL, D_FF), D_MODEL ** -0.5),
        'w_ffn_up': nrm((DEPTH, 2, D_MODEL, D_FF), D_MODEL ** -0.5),
        'w_ffn_down': nrm((DEPTH, 2, D_FF, D_MODEL), D_FF ** -0.5),
        'norm_mix': gain((DEPTH, D_MODEL)),
        'w_in_even': nrm((N_EVEN, D_MODEL, EVEN_IN), D_MODEL ** -0.5),
        'w_out_even': nrm((N_EVEN, MIX_OUT_EVEN, D_MODEL), MIX_OUT_EVEN ** -0.5),
        'mla_q_norm': gain((N_EVEN, Q_LORA)),
        'mla_w_qb': nrm((N_EVEN, Q_LORA, H_B * (NOPE_B + ROPE_B)), Q_LORA ** -0.5),
        'mla_kv_norm': gain((N_EVEN, KV_LORA)),
        'mla_w_kvb': nrm((N_EVEN, KV_LORA, H_B * (NOPE_B + V_B)), KV_LORA ** -0.5),
        'w_in_odd': nrm((N_ODD, D_MODEL, ODD_IN), D_MODEL ** -0.5),
        'w_out_odd': nrm((N_ODD, MIX_OUT_ODD, D_MODEL), MIX_OUT_ODD ** -0.5),
        'gdn_conv_w': nrm((N_ODD, CONV_W, C_CONV_CH), CONV_W ** -0.5),
        'gdn_a_log': a_log,
        'gdn_dt_bias': dt + jnp.log(-jnp.expm1(-dt)),
        'gdn_o_norm': gain((N_ODD, DV_C)),
        'norm_final': gain((D_MODEL,)),
    }


def reference(x_prompt, x_sample, cache_a_k, cache_a_v, cache_a_idx, cache_b_ckv, cache_b_krope,
              cache_d_k, cache_d_v, state_c_ssm, state_c_conv, page_table, norm_ffn, w_ffn_gate,
              w_ffn_up, w_ffn_down, norm_mix, w_in_even, w_out_even, mla_q_norm, mla_w_qb, mla_kv_norm,
              mla_w_kvb, w_in_odd, w_out_odd, gdn_conv_w, gdn_a_log, gdn_dt_bias, gdn_o_norm, norm_final):
    w = {'norm_ffn': norm_ffn, 'w_ffn_gate': w_ffn_gate, 'w_ffn_up': w_ffn_up, 'w_ffn_down': w_ffn_down,
         'norm_mix': norm_mix, 'w_in_even': w_in_even, 'w_out_even': w_out_even, 'mla_q_norm': mla_q_norm,
         'mla_w_qb': mla_w_qb, 'mla_kv_norm': mla_kv_norm, 'mla_w_kvb': mla_w_kvb, 'w_in_odd': w_in_odd,
         'w_out_odd': w_out_odd, 'gdn_conv_w': gdn_conv_w, 'gdn_a_log': gdn_a_log, 'gdn_dt_bias': gdn_dt_bias,
         'gdn_o_norm': gdn_o_norm, 'norm_final': norm_final}
    cache = {'cache_a_k': cache_a_k, 'cache_a_v': cache_a_v, 'cache_a_idx': cache_a_idx,
             'cache_b_ckv': cache_b_ckv, 'cache_b_krope': cache_b_krope, 'cache_d_k': cache_d_k,
             'cache_d_v': cache_d_v, 'state_c_ssm': state_c_ssm, 'state_c_conv': state_c_conv,
             'page_table': page_table}
    past = page_table.shape[1] * PAGE_SIZE
    y_prompt, sp = _layer_stack(x_prompt, jnp.arange(x_prompt.shape[1]), w, None)
    y_sample, ss = _layer_stack(x_sample, past + jnp.arange(x_sample.shape[1]), w, cache)
    return (y_prompt, y_sample,
            sp['a_k'], sp['a_v'], sp['a_idx'], sp['b_ckv'], sp['b_krope'], sp['d_k'], sp['d_v'], sp['c_ssm'], sp['c_conv'],
            ss['a_k'], ss['a_v'], ss['a_idx'], ss['b_ckv'], ss['b_krope'], ss['d_k'], ss['d_v'], ss['c_ssm'], ss['c_conv'])
```

```python
import functools
import math

import numpy as np
import jax
import jax.numpy as jnp
from jax import lax
from jax.experimental import pallas as pl
from jax.experimental.pallas import tpu as pltpu

F32 = jnp.float32
BF16 = jnp.bfloat16
NEG_INF = float("-inf")
INT_MIN = -(2 ** 31)

EPS = 1e-6
ROPE_THETA = 500000.0
PAGE = 128
QB = 128

H_A, HD_A, KV_A, ROT_A = 8, 64, 2, 16
H_I, D_I, ROT_I = 8, 32, 8
TOPK_MAX = 256
H_B, NOPE_B, ROPE_B, V_B = 8, 64, 32, 64
Q_LORA, KV_LORA = 256, 128
MLA_SCALE = (NOPE_B + ROPE_B) ** -0.5
H_C, DK_C, DV_C, CONV_W, CHUNK_C = 8, 64, 64, 4, 64
C_CONV_CH = H_C * (2 * DK_C + DV_C)
H_D, HD_D, KV_D = 8, 64, 4
TP = 8

VMEM_LIMIT = 56 * 2 ** 20


def _cp(sem):
    return pltpu.CompilerParams(dimension_semantics=sem, vmem_limit_bytes=VMEM_LIMIT)


def _div_tile(n, cap):
    best = None
    for t in range(8, min(n, cap) + 1, 8):
        if n % t == 0:
            best = t
    assert best is not None, (n, cap)
    return best


def _rms(x, g):
    return x * lax.rsqrt(jnp.mean(x * x, axis=-1, keepdims=True) + EPS) * g


def _dot(a, b):
    return jnp.dot(a.astype(BF16), b.astype(BF16), preferred_element_type=F32)


def _dot_nt(a, b):
    return lax.dot_general(a.astype(BF16), b.astype(BF16), (((1,), (1,)), ((), ())),
                           preferred_element_type=F32)


def _dot_tn(a, b):
    return lax.dot_general(a.astype(BF16), b.astype(BF16), (((0,), (0,)), ((), ())),
                           preferred_element_type=F32)


def _split3(a):
    hi = a.astype(BF16)
    r = a - hi.astype(F32)
    mid = r.astype(BF16)
    lo = (r - mid.astype(F32)).astype(BF16)
    return hi, mid, lo


def _dot_lsplit(a, b_exact):
    hi, mid, lo = _split3(a)
    d = lambda x: jnp.dot(x, b_exact, preferred_element_type=F32)
    return d(hi) + d(mid) + d(lo)


def _dot_rsplit(a_exact, b):
    hi, mid, lo = _split3(b)
    d = lambda x: jnp.dot(a_exact, x, preferred_element_type=F32)
    return d(hi) + d(mid) + d(lo)


def _dot3(a, b):
    ah = a.astype(BF16)
    al = (a - ah.astype(F32)).astype(BF16)
    bh = b.astype(BF16)
    bl = (b - bh.astype(F32)).astype(BF16)
    d = lambda x, y: jnp.dot(x, y, preferred_element_type=F32)
    return d(ah, bh) + d(ah, bl) + d(al, bh)


def _silu(x):
    return x * jax.nn.sigmoid(x)


def _softplus(x):
    return jnp.maximum(x, 0.0) + jnp.log1p(jnp.exp(-jnp.abs(x)))


def _log_sigmoid(x):
    return jnp.minimum(x, 0.0) - jnp.log1p(jnp.exp(-jnp.abs(x)))


def _ffn_body(x_ref, g_ref, wg_ref, wu_ref, wd_ref, o_ref, n_ref, acc_ref):
    f = pl.program_id(1)

    @pl.when(f == 0)
    def _():
        n_ref[...] = _rms(x_ref[...], g_ref[...]).astype(BF16)
        acc_ref[...] = jnp.zeros_like(acc_ref)

    n = n_ref[...]
    gt = jnp.dot(n, wg_ref[...], preferred_element_type=F32)
    up = jnp.dot(n, wu_ref[...], preferred_element_type=F32)
    h = (_silu(gt) * up).astype(BF16)
    acc_ref[...] += jnp.dot(h, wd_ref[...], preferred_element_type=F32)

    @pl.when(f == pl.num_programs(1) - 1)
    def _():
        o_ref[...] = x_ref[...] + 0.5 * acc_ref[...]


def _ffn(x, g_all, wg_all, wu_all, wd_all, l, j):
    n, d = x.shape
    dff = wg_all.shape[-1]
    tm = _div_tile(n, 1280)
    tf = 256 if dff % 256 == 0 else 128
    gi = l * 2 + j
    return pl.pallas_call(
        _ffn_body,
        grid=(n // tm, dff // tf),
        in_specs=[
            pl.BlockSpec((tm, d), lambda i, f: (i, 0)),
            pl.BlockSpec((None, 1, d), lambda i, f: (gi, 0, 0)),
            pl.BlockSpec((None, None, d, tf), lambda i, f: (l, j, 0, f)),
            pl.BlockSpec((None, None, d, tf), lambda i, f: (l, j, 0, f)),
            pl.BlockSpec((None, None, tf, d), lambda i, f: (l, j, f, 0)),
        ],
        out_specs=pl.BlockSpec((tm, d), lambda i, f: (i, 0)),
        out_shape=jax.ShapeDtypeStruct((n, d), F32),
        scratch_shapes=[pltpu.VMEM((tm, d), BF16), pltpu.VMEM((tm, d), F32)],
        compiler_params=_cp(("parallel", "arbitrary")),
        name="ffn",
    )(x, g_all, wg_all, wu_all, wd_all)


def _rope_parts(pos, hd, rot):
    half = rot // 2
    inv = ROPE_THETA ** (-jnp.arange(half, dtype=F32) / half)
    ang = pos.astype(F32)[:, None] * inv[None, :]
    cos, sin = jnp.cos(ang), jnp.sin(ang)
    p = pos.shape[0]
    c = jnp.concatenate([cos, cos, jnp.ones((p, hd - rot), F32)], axis=1)
    sa = jnp.concatenate([-sin, jnp.zeros((p, hd - half), F32)], axis=1)
    sb = jnp.concatenate([jnp.zeros((p, half), F32), sin, jnp.zeros((p, hd - rot), F32)], axis=1)
    return c, sa, sb


def _rope_table(pos, hd, rot):
    return jnp.concatenate([jnp.tile(t, (1, 128 // hd)) for t in _rope_parts(pos, hd, rot)], axis=1)


def _small_table(pos):
    p = pos.shape[0]
    ci, sai, sbi = _rope_parts(pos, D_I, ROT_I)
    cb, sab, sbb = _rope_parts(pos, ROPE_B, ROPE_B)
    z = lambda w: jnp.zeros((p, w), F32)
    scale = jnp.full((p, H_I), (H_I * D_I) ** -0.5, F32)
    return jnp.concatenate([
        ci, cb, scale, z(56),
        sai, z(96), sbi, z(96),
        z(32), sab, z(64), z(32), sbb, z(64)], axis=1)


def _rope_lanes(x, tab, half):
    w = x.shape[-1]
    rep = w // 128
    parts = [tab[:, 0:128], tab[:, 128:256], tab[:, 256:384]]
    if rep > 1:
        parts = [jnp.concatenate([t] * rep, axis=1) for t in parts]
    c, sa, sb = parts
    return x * c + pltpu.roll(x, w - half, 1) * sa + pltpu.roll(x, half, 1) * sb


def _proj_even_body(x_ref, g_ref, w_ref, ta_ref, ti_ref, tb_ref, ts_ref, qng_ref, wqb_ref, kvg_ref,
                    wkvb_ref, qa_ref, ka_ref, va_ref, kva_ref, qi_ref, sm_ref, qn_ref, qr_ref, c_ref,
                    kn_ref, vb_ref):
    n = _rms(x_ref[...], g_ref[...]).astype(BF16)
    z = jnp.dot(n, w_ref[...], preferred_element_type=F32)
    ta = ta_ref[...]
    qa_ref[...] = _rope_lanes(z[:, 0:512], ta, ROT_A // 2).astype(BF16)
    ka = _rope_lanes(z[:, 512:640], ta, ROT_A // 2)
    va = z[:, 640:768]
    ka_ref[...] = ka
    va_ref[...] = va
    kva_ref[:, 0:128] = ka.astype(BF16)
    kva_ref[:, 128:256] = va.astype(BF16)
    qi_ref[...] = _rope_lanes(z[:, 768:1024], ti_ref[...], ROT_I // 2).astype(BF16)
    sm = z[:, 1408:1536]
    ts = ts_ref[...]
    sm_ref[...] = (sm * ts[:, 0:128]
                   + pltpu.roll(sm, 128 - ROT_I // 2, 1) * ts[:, 128:256]
                   + pltpu.roll(sm, ROT_I // 2, 1) * ts[:, 256:384]
                   + pltpu.roll(sm, 128 - ROPE_B // 2, 1) * ts[:, 384:512]
                   + pltpu.roll(sm, ROPE_B // 2, 1) * ts[:, 512:640])
    qln = _rms(z[:, 1024:1280], qng_ref[...]).astype(BF16)
    qb = jnp.dot(qln, wqb_ref[...], preferred_element_type=F32)
    qn_ref[...] = qb[:, 0:512].astype(BF16)
    qr_ref[...] = _rope_lanes(qb[:, 512:768], tb_ref[...], ROPE_B // 2).astype(BF16)
    c = _rms(z[:, 1280:1408], kvg_ref[...])
    c_ref[...] = c
    kvb = jnp.dot(c.astype(BF16), wkvb_ref[...], preferred_element_type=F32)
    kn_ref[...] = kvb[:, 0:512].astype(BF16)
    vb_ref[...] = kvb[:, 512:1024].astype(BF16)


def _proj_even(x, g, w, tabs, qng, wqb, kvg, wkvb, tm, tab_index):
    n, d = x.shape
    ta, ti, tb, ts = tabs
    row = lambda width: pl.BlockSpec((tm, width), lambda i: (i, 0))
    full = lambda a: pl.BlockSpec(a.shape, lambda i: (0,) * a.ndim)
    tabspec = lambda a: pl.BlockSpec((tm, a.shape[1]), lambda i: (tab_index(i), 0))
    widths = [(512, BF16), (128, F32), (128, F32), (256, BF16), (256, BF16), (128, F32), (512, BF16),
              (256, BF16), (128, F32), (512, BF16), (512, BF16)]
    return pl.pallas_call(
        _proj_even_body,
        grid=(n // tm,),
        in_specs=[row(d), full(g), full(w), tabspec(ta), tabspec(ti), tabspec(tb), tabspec(ts),
                  full(qng), full(wqb), full(kvg), full(wkvb)],
        out_specs=[row(wd) for wd, _ in widths],
        out_shape=[jax.ShapeDtypeStruct((n, wd), dt) for wd, dt in widths],
        compiler_params=_cp(("parallel",)),
        name="proj_even",
    )(x, g, w, ta, ti, tb, ts, qng, wqb, kvg, wkvb)


def _proj_odd_body(x_ref, g_ref, w_ref, qkv_ref, gate_ref, qd_ref, kd_ref, vd_ref, kvd_ref, sm_ref):
    n = _rms(x_ref[...], g_ref[...]).astype(BF16)
    z = jnp.dot(n, w_ref[...], preferred_element_type=F32)
    qkv_ref[...] = z[:, 0:1536]
    gate_ref[...] = z[:, 1536:2048]
    qd_ref[...] = z[:, 2048:2560].astype(BF16)
    kd_ref[...] = z[:, 2560:2816]
    vd_ref[...] = z[:, 2816:3072]
    kvd_ref[...] = z[:, 2560:3072].astype(BF16)
    sm_ref[...] = z[:, 3072:3200]


def _proj_odd(x, g, w, tm):
    n, d = x.shape
    row = lambda width: pl.BlockSpec((tm, width), lambda i: (i, 0))
    full = lambda a: pl.BlockSpec(a.shape, lambda i: (0,) * a.ndim)
    widths = [(1536, F32), (512, F32), (512, BF16), (256, F32), (256, F32), (512, BF16), (128, F32)]
    return pl.pallas_call(
        _proj_odd_body,
        grid=(n // tm,),
        in_specs=[row(d), full(g), full(w)],
        out_specs=[row(wd) for wd, _ in widths],
        out_shape=[jax.ShapeDtypeStruct((n, wd), dt) for wd, dt in widths],
        compiler_params=_cp(("parallel",)),
        name="proj_odd",
    )(x, g, w)


def _select_bias(sc, valid, k, key_ref, tri):
    r, w = sc.shape
    sc = jnp.where(sc == 0.0, 0.0, sc)
    bits = lax.bitcast_convert_type(sc, jnp.int32)
    key = jnp.where(bits < 0, bits ^ jnp.int32(0x7FFFFFFF), bits)
    key_ref[...] = jnp.where(valid, key, jnp.int32(INT_MIN))
    kf = float(k)

    def count_ge(t):
        return jnp.sum(jnp.where(key_ref[...] >= t, 1.0, 0.0), axis=1, keepdims=True)

    t0 = jnp.where(count_ge(jnp.zeros((r, 1), jnp.int32)) >= kf, jnp.int32(0), jnp.int32(INT_MIN))

    def body(i, t):
        cand = t | jnp.left_shift(jnp.int32(1), jnp.int32(30) - i)
        return jnp.where(count_ge(cand) >= kf, cand, t)

    t = lax.fori_loop(0, 31, body, t0)
    key = key_ref[...]
    gt = key > t
    eq = jnp.logical_and(key == t, valid)
    need = kf - jnp.sum(jnp.where(gt, 1.0, 0.0), axis=1, keepdims=True)
    carry = jnp.zeros((r, 1), F32)
    parts = []
    for c in range(w // 128):
        sl = slice(c * 128, (c + 1) * 128)
        eqc = jnp.where(eq[:, sl], 1.0, 0.0).astype(BF16)
        pre = jnp.dot(eqc, tri, preferred_element_type=F32) + carry
        carry = pre[:, 127:128]
        keep = jnp.logical_or(gt[:, sl], jnp.logical_and(eq[:, sl], pre <= need))
        parts.append(jnp.where(keep, 0.0, NEG_INF))
    return jnp.concatenate(parts, axis=1)


def _softmax_pv(s, v):
    m = jnp.max(s, axis=1, keepdims=True)
    p = jnp.exp(s - m)
    l = jnp.sum(p, axis=1, keepdims=True)
    return jnp.dot(p.astype(BF16), v, preferred_element_type=F32) / l


def _attn_even_prompt_body(qa_ref, qi_ref, smq_ref, kit_ref, kva_ref, qn_ref, qr_ref, kn_ref, kr_ref,
                           vb_ref, tri_ref, o_ref, key_ref, *, n_top, seq):
    i = pl.program_id(1)
    rows = i * QB + lax.broadcasted_iota(jnp.int32, (QB, seq), 0)
    cols = lax.broadcasted_iota(jnp.int32, (QB, seq), 1)
    causal = cols <= rows
    kit = kit_ref[...]
    sc = jnp.zeros((QB, seq), F32)
    for h in range(H_I):
        d = jnp.dot(qi_ref[:, h * D_I:(h + 1) * D_I], kit, preferred_element_type=F32)
        sc = sc + smq_ref[:, 64 + h:65 + h] * jnp.maximum(d, 0.0)
    bias = _select_bias(sc, causal, n_top, key_ref, tri_ref[...])
    rep = H_A // KV_A
    for h in range(H_A):
        g = h // rep
        s = _dot_nt(qa_ref[:, h * HD_A:(h + 1) * HD_A], kva_ref[:, g * HD_A:(g + 1) * HD_A])
        s = s * HD_A ** -0.5 + bias
        o = _softmax_pv(s, kva_ref[:, 128 + g * HD_A:128 + (g + 1) * HD_A])
        o_ref[:, h * HD_A:(h + 1) * HD_A] = o.astype(BF16)
    cbias = jnp.where(causal, 0.0, NEG_INF)
    kr = kr_ref[...]
    for h in range(H_B):
        s = (_dot_nt(qn_ref[:, h * NOPE_B:(h + 1) * NOPE_B], kn_ref[:, h * NOPE_B:(h + 1) * NOPE_B])
             + _dot_nt(qr_ref[:, h * ROPE_B:(h + 1) * ROPE_B], kr))
        s = s * MLA_SCALE + cbias
        o = _softmax_pv(s, vb_ref[:, h * V_B:(h + 1) * V_B])
        o_ref[:, 512 + h * V_B:512 + (h + 1) * V_B] = o.astype(BF16)


def _attn_even_prompt(qa, qi, sm, kit, kva, qn, qr, kn, kr, vb, tri, nb, seq, n_top):
    nqb = seq // QB
    qrow = lambda w: pl.BlockSpec((QB, w), lambda b, i: (b * nqb + i, 0))
    krow = lambda w: pl.BlockSpec((seq, w), lambda b, i: (b, 0))
    return pl.pallas_call(
        functools.partial(_attn_even_prompt_body, n_top=n_top, seq=seq),
        grid=(nb, nqb),
        in_specs=[qrow(512), qrow(256), qrow(128),
                  pl.BlockSpec((None, D_I, seq), lambda b, i: (b, 0, 0)),
                  krow(256), qrow(512), qrow(256), krow(512),
                  pl.BlockSpec((None, seq, ROPE_B), lambda b, i: (b, 0, 0)),
                  krow(512),
                  pl.BlockSpec((128, 128), lambda b, i: (0, 0))],
        out_specs=qrow(1024),
        out_shape=jax.ShapeDtypeStruct((nb * seq, 1024), BF16),
        scratch_shapes=[pltpu.VMEM((QB, seq), jnp.int32)],
        compiler_params=_cp(("parallel", "arbitrary")),
        name="attn_even_prompt",
    )(qa, qi, sm, kit, kva, qn, qr, kn, kr, vb, tri)


def _suffix_rest(lk, ms, cw, carry):
    w = lk.shape[1]
    parts = [None] * (w // cw)
    for c in reversed(range(w // cw)):
        lkc = lk[:, c * cw:(c + 1) * cw]
        r = _dot_lsplit(lkc, ms) + carry
        carry = r[:, 0:1] + lkc[:, 0:1]
        parts[c] = r
    rest = parts[0] if len(parts) == 1 else jnp.concatenate(parts, axis=1)
    return rest, carry


def _sb_prompt_body(q_ref, kv_ref, ms_ref, o_ref, *, seq, cw):
    i = pl.program_id(1)
    rows = i * QB + lax.broadcasted_iota(jnp.int32, (QB, seq), 0)
    cols = lax.broadcasted_iota(jnp.int32, (QB, seq), 1)
    mask = cols < rows
    ms = ms_ref[...]
    rep = H_D // KV_D
    for h in range(H_D):
        g = h // rep
        z = _dot_nt(q_ref[:, h * HD_D:(h + 1) * HD_D], kv_ref[:, g * HD_D:(g + 1) * HD_D]) * HD_D ** -0.5
        ls = _log_sigmoid(z)
        lk = jnp.where(mask, ls - z, 0.0)
        rest, _ = _suffix_rest(lk, ms, cw, jnp.zeros((QB, 1), F32))
        a = jnp.where(mask, jnp.exp(ls + rest), 0.0)
        o = jnp.dot(a.astype(BF16), kv_ref[:, 256 + g * HD_D:256 + (g + 1) * HD_D],
                    preferred_element_type=F32)
        o_ref[:, h * HD_D:(h + 1) * HD_D] = o.astype(BF16)


def _sb_prompt(qd, kvd, nb, seq):
    nqb = seq // QB
    cw = 256 if seq % 256 == 0 else 128
    ms = jnp.asarray(np.tril(np.ones((cw, cw), np.float32), -1), BF16)
    return pl.pallas_call(
        functools.partial(_sb_prompt_body, seq=seq, cw=cw),
        grid=(nb, nqb),
        in_specs=[pl.BlockSpec((QB, 512), lambda b, i: (b * nqb + i, 0)),
                  pl.BlockSpec((seq, 512), lambda b, i: (b, 0)),
                  pl.BlockSpec((cw, cw), lambda b, i: (0, 0))],
        out_specs=pl.BlockSpec((QB, 512), lambda b, i: (b * nqb + i, 0)),
        out_shape=jax.ShapeDtypeStruct((nb * seq, 512), BF16),
        compiler_params=_cp(("parallel", "arbitrary")),
        name="sb_prompt",
    )(qd, kvd, ms)


def _gdn_body(qkv_ref, sm_ref, gate_ref, cw_ref, al_ref, dtb_ref, alc_ref, dtc_ref, on_ref, bd_ref,
              tril_ref, conv0_ref, s0_ref, o_ref, sout_ref, cout_ref, xbuf, s_scr, *, valid):
    C = CHUNK_C
    c = pl.program_id(1)
    nc = pl.num_programs(1)

    @pl.when(c == 0)
    def _():
        xbuf[0:8, :] = conv0_ref[...]
        s_scr[...] = s0_ref[...]

    xbuf[8:8 + C, :] = qkv_ref[...]
    y = (cw_ref[3:4, :] * xbuf[8:8 + C, :] + cw_ref[2:3, :] * xbuf[7:7 + C, :]
         + cw_ref[1:2, :] * xbuf[6:6 + C, :] + cw_ref[0:1, :] * xbuf[5:5 + C, :])
    xc = _silu(y)

    @pl.when(c == nc - 1)
    def _():
        cout_ref[...] = xbuf[8 + valid - (CONV_W - 1):8 + valid, :]

    xbuf[0:8, :] = xbuf[C:C + 8, :]

    hk = H_C * DK_C
    q, k, v = xc[:, 0:hk], xc[:, hk:2 * hk], xc[:, 2 * hk:]
    bd = bd_ref[...]
    qn = q * lax.rsqrt(_dot_lsplit(q * q, bd) + EPS) * DK_C ** -0.5
    kn = k * lax.rsqrt(_dot_lsplit(k * k, bd) + EPS)

    sm = sm_ref[...]
    smt = sm.T
    g = -jnp.exp(al_ref[...]) * _softplus(sm + dtb_ref[...])
    gt = -jnp.exp(alc_ref[...]) * _softplus(smt[0:H_C, :] + dtc_ref[...])
    beta = jax.nn.sigmoid(sm)
    if valid < C:
        g = jnp.where(lax.broadcasted_iota(jnp.int32, g.shape, 0) < valid, g, 0.0)
        gt = jnp.where(lax.broadcasted_iota(jnp.int32, gt.shape, 1) < valid, gt, 0.0)
        beta = jnp.where(lax.broadcasted_iota(jnp.int32, beta.shape, 0) < valid, beta, 0.0)
    tril = tril_ref[...]
    gc = _dot_rsplit(tril, g)
    gct = _dot_nt_lsplit(gt, tril)

    ri = lax.broadcasted_iota(jnp.int32, (C, C), 0)
    ci = lax.broadcasted_iota(jnp.int32, (C, C), 1)
    incl = ri >= ci
    strict = ri > ci
    for h in range(H_C):
        sl = slice(h * DK_C, (h + 1) * DK_C)
        gcol = gc[:, h:h + 1]
        grow = gct[h:h + 1, :]
        gam = jnp.exp(jnp.where(incl, gcol - grow, NEG_INF))
        kh, qh, vh = kn[:, sl], qn[:, sl], v[:, sl]
        bh = beta[:, H_C + h:H_C + h + 1]
        kb = kh * bh
        neg_a = jnp.where(strict, -(_dot_nt(kb, kh) * gam), 0.0)
        x = jnp.concatenate([vh * bh, kb * jnp.exp(gcol)], axis=1)
        pw = neg_a
        for lvl in range(6):
            x = x + _dot3(pw, x)
            if lvl < 5:
                pw = _dot3(pw, pw)
        u, w = x[:, 0:DV_C], x[:, DV_C:]
        a_qk = _dot_nt(qh, kh) * gam
        sh = s_scr[h]
        v_new = u - _dot(w, sh)
        oh = _dot(qh * jnp.exp(gcol), sh) + _dot(a_qk, v_new)
        glast = gcol[C - 1:C, :]
        k_dec = kh * jnp.exp(glast - gcol)
        s_scr[h] = sh * jnp.exp(glast) + _dot_tn(k_dec, v_new)
        on = _rms(oh, on_ref[...])
        o_ref[:, sl] = (on * _silu(gate_ref[:, sl])).astype(BF16)

    @pl.when(c == nc - 1)
    def _():
        sout_ref[...] = s_scr[...]


def _dot_nt_lsplit(a, b_exact):
    hi, mid, lo = _split3(a)
    d = lambda x: lax.dot_general(x, b_exact, (((1,), (1,)), ((), ())), preferred_element_type=F32)
    return d(hi) + d(mid) + d(lo)


def _gdn(qkv, sm, gate, conv_w, a_log, dt_bias, o_norm, conv0, s0, ng, nchunk, valid):
    C = CHUNK_C
    pad128 = lambda a: jnp.pad(a.reshape(1, -1), ((0, 0), (0, 128 - a.size)))
    al, dtb = pad128(a_log), pad128(dt_bias)
    alc, dtc = a_log.reshape(H_C, 1), dt_bias.reshape(H_C, 1)
    on = o_norm.reshape(1, DV_C)
    hk = H_C * DK_C
    bd = jnp.asarray(np.kron(np.eye(H_C, dtype=np.float32), np.ones((DK_C, DK_C), np.float32)), BF16)
    tril = jnp.asarray(np.tril(np.ones((C, C), np.float32)), BF16)
    row = lambda w: pl.BlockSpec((C, w), lambda b, c: (b * nchunk + c, 0))
    full = lambda a: pl.BlockSpec(a.shape, lambda b, c: (0,) * a.ndim)
    return pl.pallas_call(
        functools.partial(_gdn_body, valid=valid),
        grid=(ng, nchunk),
        in_specs=[row(3 * hk), row(128), row(hk), full(conv_w), full(al), full(dtb), full(alc), full(dtc),
                  full(on), full(bd), full(tril),
                  pl.BlockSpec((None, 8, 3 * hk), lambda b, c: (b, 0, 0)),
                  pl.BlockSpec((None, H_C, DK_C, DV_C), lambda b, c: (b, 0, 0, 0))],
        out_specs=[row(hk),
                   pl.BlockSpec((None, H_C, DK_C, DV_C), lambda b, c: (b, 0, 0, 0)),
                   pl.BlockSpec((None, CONV_W - 1, 3 * hk), lambda b, c: (b, 0, 0))],
        out_shape=[jax.ShapeDtypeStruct((ng * nchunk * C, hk), BF16),
                   jax.ShapeDtypeStruct((ng, H_C, DK_C, DV_C), F32),
                   jax.ShapeDtypeStruct((ng, CONV_W - 1, 3 * hk), F32)],
        scratch_shapes=[pltpu.VMEM((C + 8, 3 * hk), F32), pltpu.VMEM((H_C, DK_C, DV_C), F32)],
        compiler_params=_cp(("parallel", "arbitrary")),
        name="gdn",
    )(qkv, sm, gate, conv_w, al, dtb, alc, dtc, on, bd, tril, conv0, s0)


def _outproj_body(x_ref, a1_ref, a2_ref, w_ref, o_ref):
    h = a1_ref.shape[1]
    o_ref[...] = (x_ref[...] + jnp.dot(a1_ref[...], w_ref[0:h, :], preferred_element_type=F32)
                  + jnp.dot(a2_ref[...], w_ref[h:, :], preferred_element_type=F32))


def _outproj(x, a1, a2, w, c1, c2):
    n, d = x.shape
    tm = _div_tile(n, 1280)
    return pl.pallas_call(
        _outproj_body,
        grid=(n // tm,),
        in_specs=[pl.BlockSpec((tm, d), lambda i: (i, 0)),
                  pl.BlockSpec((tm, 512), lambda i: (i, c1)),
                  pl.BlockSpec((tm, 512), lambda i: (i, c2)),
                  pl.BlockSpec(w.shape, lambda i: (0, 0))],
        out_specs=pl.BlockSpec((tm, d), lambda i: (i, 0)),
        out_shape=jax.ShapeDtypeStruct((n, d), F32),
        compiler_params=_cp(("parallel",)),
        name="outproj",
    )(x, a1, a2, w)


def _final_norm_body(x_ref, g_ref, o_ref):
    o_ref[...] = _rms(x_ref[...], g_ref[...])


def _final_norm(x, g):
    n, d = x.shape
    tm = _div_tile(n, 1280)
    return pl.pallas_call(
        _final_norm_body,
        grid=(n // tm,),
        in_specs=[pl.BlockSpec((tm, d), lambda i: (i, 0)), pl.BlockSpec((1, d), lambda i: (0, 0))],
        out_specs=pl.BlockSpec((tm, d), lambda i: (i, 0)),
        out_shape=jax.ShapeDtypeStruct((n, d), F32),
        compiler_params=_cp(("parallel",)),
        name="final_norm",
    )(x, g)


def _page_specs(shape_tail, layer, n_pages, pp, nsteps, reverse=False):
    specs = []
    for p in range(pp):
        def imap(b, st, pt, p=p):
            if reverse:
                lp = n_pages - 1 - (jnp.maximum(st - 1, 0) * pp + p)
            else:
                lp = jnp.minimum(st, nsteps - 1) * pp + p
            return (layer, pt[b * n_pages + lp]) + (0,) * len(shape_tail)
        specs.append(pl.BlockSpec((None, None) + shape_tail, imap))
    return specs


def _idx_sample_body(pt_ref, qi_ref, wi_ref, knew_ref, *rest, pp):
    pages = rest[:pp]
    sp_ref, sn_ref = rest[pp], rest[pp + 1]
    st = pl.program_id(1)
    ns = pl.num_programs(1)
    qi = qi_ref[...]
    wi = wi_ref[...]

    def score(kpage):
        d = _dot_nt(qi, kpage)
        acc = jnp.zeros((TP, PAGE), F32)
        for h in range(H_I):
            acc = acc + wi[:, h:h + 1] * jnp.maximum(d[h * TP:(h + 1) * TP, :], 0.0)
        return acc

    @pl.when(st < ns - 1)
    def _():
        for p in range(pp):
            sp_ref[:, p * PAGE:(p + 1) * PAGE] = score(pages[p][...])

    @pl.when(st == ns - 1)
    def _():
        sn_ref[...] = score(knew_ref[...])


def _idx_sample(pt, qi_s, wi_s, ki_new, pool_i, layer, ndb, n_pages, pp):
    nsteps = n_pages // pp
    grid_spec = pltpu.PrefetchScalarGridSpec(
        num_scalar_prefetch=1,
        grid=(ndb, nsteps + 1),
        in_specs=[pl.BlockSpec((None, H_I * TP, D_I), lambda b, st, pt: (b, 0, 0)),
                  pl.BlockSpec((None, TP, H_I), lambda b, st, pt: (b, 0, 0)),
                  pl.BlockSpec((None, PAGE, D_I), lambda b, st, pt: (b, 0, 0))]
        + _page_specs((PAGE, D_I), layer, n_pages, pp, nsteps),
        out_specs=[pl.BlockSpec((None, TP, pp * PAGE), lambda b, st, pt: (b, 0, jnp.minimum(st, nsteps - 1))),
                   pl.BlockSpec((None, TP, PAGE), lambda b, st, pt: (b, 0, 0))],
    )
    return pl.pallas_call(
        functools.partial(_idx_sample_body, pp=pp),
        grid_spec=grid_spec,
        out_shape=[jax.ShapeDtypeStruct((ndb, TP, n_pages * PAGE), F32),
                   jax.ShapeDtypeStruct((ndb, TP, PAGE), F32)],
        compiler_params=_cp(("parallel", "arbitrary")),
        name="idx_sample",
    )(pt, qi_s, wi_s, ki_new, *([pool_i] * pp))


def _select_sample_body(sp_ref, sn_ref, tri_ref, bp_ref, bn_ref, key_ref, *, n_top, past, t_new):
    sc = jnp.concatenate([sp_ref[...], sn_ref[...]], axis=1)
    r, w = sc.shape
    t = lax.broadcasted_iota(jnp.int32, (r, w), 0) & (TP - 1)
    cols = lax.broadcasted_iota(jnp.int32, (r, w), 1)
    valid = jnp.logical_or(cols < past, jnp.logical_and(cols - past <= t, cols - past < t_new))
    bias = _select_bias(sc, valid, n_top, key_ref, tri_ref[...])
    bp_ref[...] = bias[:, 0:past]
    bn_ref[...] = bias[:, past:]


def _select_sample(sp, sn, tri, n_top, t_new):
    rows, past = sp.shape
    tr = _div_tile(rows, 64)
    return pl.pallas_call(
        functools.partial(_select_sample_body, n_top=n_top, past=past, t_new=t_new),
        grid=(rows // tr,),
        in_specs=[pl.BlockSpec((tr, past), lambda i: (i, 0)),
                  pl.BlockSpec((tr, PAGE), lambda i: (i, 0)),
                  pl.BlockSpec((128, 128), lambda i: (0, 0))],
        out_specs=[pl.BlockSpec((tr, past), lambda i: (i, 0)),
                   pl.BlockSpec((tr, PAGE), lambda i: (i, 0))],
        out_shape=[jax.ShapeDtypeStruct((rows, past), F32), jax.ShapeDtypeStruct((rows, PAGE), F32)],
        scratch_shapes=[pltpu.VMEM((tr, past + PAGE), jnp.int32)],
        compiler_params=_cp(("parallel",)),
        name="select_sample",
    )(sp, sn, tri)


def _online_update(m_ref, l_ref, acc_ref, rows, s, v):
    m_old = m_ref[rows, :]
    m_new = jnp.maximum(m_old, jnp.max(s, axis=1, keepdims=True))
    m_safe = jnp.where(m_new == NEG_INF, 0.0, m_new)
    p = jnp.exp(s - m_safe)
    alpha = jnp.exp(m_old - m_safe)
    l_ref[rows, :] = alpha * l_ref[rows, :] + jnp.sum(p, axis=1, keepdims=True)
    acc_ref[rows, :] = alpha * acc_ref[rows, :] + jnp.dot(p.astype(BF16), v, preferred_element_type=F32)
    m_ref[rows, :] = m_new


def _attn_even_sample_body(pt_ref, qa_ref, qn_ref, qr_ref, bp_ref, bn_ref, kvn_ref, cn_ref, krn_ref,
                           wk_ref, wv_ref, *rest, pp, t_new):
    kpg, vpg, cpg, rpg = rest[0:pp], rest[pp:2 * pp], rest[2 * pp:3 * pp], rest[3 * pp:4 * pp]
    o_ref = rest[4 * pp]
    qlat, m_a, l_a, acc_a, m_b, l_b, acc_b = rest[4 * pp + 1:]
    st = pl.program_id(1)
    ns = pl.num_programs(1)
    rep = H_A // KV_A
    rows_g = rep * TP

    @pl.when(st == 0)
    def _():
        for h in range(H_B):
            qlat[h * TP:(h + 1) * TP, :] = _dot_nt(qn_ref[h * TP:(h + 1) * TP, :], wk_ref[h]).astype(BF16)
        m_a[...] = jnp.full_like(m_a, NEG_INF)
        m_b[...] = jnp.full_like(m_b, NEG_INF)
        l_a[...] = jnp.zeros_like(l_a)
        l_b[...] = jnp.zeros_like(l_b)
        acc_a[...] = jnp.zeros_like(acc_a)
        acc_b[...] = jnp.zeros_like(acc_b)

    def update(k, v, c, kr, bias8, mla_bias):
        k, v, c, kr = k.astype(BF16), v.astype(BF16), c.astype(BF16), kr.astype(BF16)
        bias = jnp.concatenate([bias8] * rep, axis=0)
        for g in range(KV_A):
            rows = slice(g * rows_g, (g + 1) * rows_g)
            s = _dot_nt(qa_ref[rows, :], k[:, g * HD_A:(g + 1) * HD_A]) * HD_A ** -0.5 + bias
            _online_update(m_a, l_a, acc_a, rows, s, v[:, g * HD_A:(g + 1) * HD_A])
        s = (_dot_nt(qlat[...], c) + _dot_nt(qr_ref[...], kr)) * MLA_SCALE
        if mla_bias is not None:
            s = s + mla_bias
        _online_update(m_b, l_b, acc_b, slice(0, H_B * TP), s, c)

    @pl.when(st < ns - 1)
    def _():
        for p in range(pp):
            update(kpg[p][...], vpg[p][...], cpg[p][...], rpg[p][...],
                   bp_ref[:, p * PAGE:(p + 1) * PAGE], None)

    @pl.when(st == ns - 1)
    def _():
        t = lax.broadcasted_iota(jnp.int32, (H_B * TP, PAGE), 0) & (TP - 1)
        j = lax.broadcasted_iota(jnp.int32, (H_B * TP, PAGE), 1)
        ok = jnp.logical_and(j <= t, j < t_new)
        update(kvn_ref[:, 0:128], kvn_ref[:, 128:256], cn_ref[...], krn_ref[...], bn_ref[...],
               jnp.where(ok, 0.0, NEG_INF))
        oa = acc_a[...] / l_a[...]
        ol = (acc_b[...] / l_b[...]).astype(BF16)
        for h in range(H_A):
            o_ref[:, h * HD_A:(h + 1) * HD_A] = oa[h * TP:(h + 1) * TP, :].astype(BF16)
        for h in range(H_B):
            ob = jnp.dot(ol[h * TP:(h + 1) * TP, :], wv_ref[h], preferred_element_type=F32)
            o_ref[:, 512 + h * V_B:512 + (h + 1) * V_B] = ob.astype(BF16)


def _attn_even_sample(pt, qa_s, qn_s, qr_s, bias_p, bias_n, kv_new, c_new, kr_new, wk, wv,
                      pool_k, pool_v, pool_c, pool_r, layer, ndb, n_pages, pp, t_new):
    nsteps = n_pages // pp
    seq3 = lambda a: pl.BlockSpec((None,) + a.shape[1:], lambda b, st, pt: (b, 0, 0))
    full3 = lambda a: pl.BlockSpec(a.shape, lambda b, st, pt: (0, 0, 0))
    grid_spec = pltpu.PrefetchScalarGridSpec(
        num_scalar_prefetch=1,
        grid=(ndb, nsteps + 1),
        in_specs=[seq3(qa_s), seq3(qn_s), seq3(qr_s),
                  pl.BlockSpec((None, TP, pp * PAGE), lambda b, st, pt: (b, 0, jnp.minimum(st, nsteps - 1))),
                  seq3(bias_n), seq3(kv_new), seq3(c_new), seq3(kr_new), full3(wk), full3(wv)]
        + _page_specs((PAGE, KV_A * HD_A), layer, n_pages, pp, nsteps)
        + _page_specs((PAGE, KV_A * HD_A), layer, n_pages, pp, nsteps)
        + _page_specs((PAGE, KV_LORA), layer, n_pages, pp, nsteps)
        + _page_specs((PAGE, ROPE_B), layer, n_pages, pp, nsteps),
        out_specs=pl.BlockSpec((None, TP, 1024), lambda b, st, pt: (b, 0, 0)),
        scratch_shapes=[pltpu.VMEM((H_B * TP, KV_LORA), BF16),
                        pltpu.VMEM((H_A * TP, 1), F32), pltpu.VMEM((H_A * TP, 1), F32),
                        pltpu.VMEM((H_A * TP, HD_A), F32),
                        pltpu.VMEM((H_B * TP, 1), F32), pltpu.VMEM((H_B * TP, 1), F32),
                        pltpu.VMEM((H_B * TP, KV_LORA), F32)],
    )
    return pl.pallas_call(
        functools.partial(_attn_even_sample_body, pp=pp, t_new=t_new),
        grid_spec=grid_spec,
        out_shape=jax.ShapeDtypeStruct((ndb, TP, 1024), BF16),
        compiler_params=_cp(("parallel", "arbitrary")),
        name="attn_even_sample",
    )(pt, qa_s, qn_s, qr_s, bias_p, bias_n, kv_new, c_new, kr_new, wk, wv,
      *([pool_k] * pp), *([pool_v] * pp), *([pool_c] * pp), *([pool_r] * pp))


def _sb_sample_body(pt_ref, q_ref, kvn_ref, ms_ref, *rest, pp, t_new):
    kpg, vpg = rest[0:pp], rest[pp:2 * pp]
    o_ref = rest[2 * pp]
    carry, acc = rest[2 * pp + 1:]
    st = pl.program_id(1)
    ns = pl.num_programs(1)
    rep = H_D // KV_D
    rows_g = rep * TP
    ms = ms_ref[...]

    def update(k, v, mask):
        k, v = k.astype(BF16), v.astype(BF16)
        for g in range(KV_D):
            rows = slice(g * rows_g, (g + 1) * rows_g)
            z = _dot_nt(q_ref[rows, :], k[:, g * HD_D:(g + 1) * HD_D]) * HD_D ** -0.5
            ls = _log_sigmoid(z)
            lk = ls - z
            if mask is not None:
                lk = jnp.where(mask, lk, 0.0)
            rest_, c_new = _suffix_rest(lk, ms, PAGE, carry[rows, :])
            carry[rows, :] = c_new
            a = jnp.exp(ls + rest_)
            if mask is not None:
                a = jnp.where(mask, a, 0.0)
            acc[rows, :] += jnp.dot(a.astype(BF16), v[:, g * HD_D:(g + 1) * HD_D],
                                    preferred_element_type=F32)

    @pl.when(st == 0)
    def _():
        carry[...] = jnp.zeros_like(carry)
        acc[...] = jnp.zeros_like(acc)
        t = lax.broadcasted_iota(jnp.int32, (rows_g, PAGE), 0) & (TP - 1)
        j = lax.broadcasted_iota(jnp.int32, (rows_g, PAGE), 1)
        update(kvn_ref[:, 0:256], kvn_ref[:, 256:512], jnp.logical_and(j < t, j < t_new))

    @pl.when(st > 0)
    def _():
        for p in range(pp):
            update(kpg[p][...], vpg[p][...], None)

    @pl.when(st == ns - 1)
    def _():
        a = acc[...]
        for h in range(H_D):
            o_ref[:, h * HD_D:(h + 1) * HD_D] = a[h * TP:(h + 1) * TP, :].astype(BF16)


def _sb_sample(pt, qd_s, kvd_new, pool_k, pool_v, layer, ndb, n_pages, pp, t_new):
    nsteps = n_pages // pp
    ms = jnp.asarray(np.tril(np.ones((PAGE, PAGE), np.float32), -1), BF16)
    seq3 = lambda a: pl.BlockSpec((None,) + a.shape[1:], lambda b, st, pt: (b, 0, 0))
    grid_spec = pltpu.PrefetchScalarGridSpec(
        num_scalar_prefetch=1,
        grid=(ndb, nsteps + 1),
        in_specs=[seq3(qd_s), seq3(kvd_new), pl.BlockSpec((PAGE, PAGE), lambda b, st, pt: (0, 0))]
        + _page_specs((PAGE, KV_D * HD_D), layer, n_pages, pp, nsteps, reverse=True)
        + _page_specs((PAGE, KV_D * HD_D), layer, n_pages, pp, nsteps, reverse=True),
        out_specs=pl.BlockSpec((None, TP, 512), lambda b, st, pt: (b, 0, 0)),
        scratch_shapes=[pltpu.VMEM((H_D * TP, 1), F32), pltpu.VMEM((H_D * TP, HD_D), F32)],
    )
    return pl.pallas_call(
        functools.partial(_sb_sample_body, pp=pp, t_new=t_new),
        grid_spec=grid_spec,
        out_shape=jax.ShapeDtypeStruct((ndb, TP, 512), BF16),
        compiler_params=_cp(("parallel", "arbitrary")),
        name="sb_sample",
    )(pt, qd_s, kvd_new, ms, *([pool_k] * pp), *([pool_v] * pp))


def _rows_ht(a, ndb, t, heads, d):
    a = a.reshape(ndb, t, heads, d).transpose(0, 2, 1, 3)
    a = jnp.pad(a, ((0, 0), (0, 0), (0, TP - t), (0, 0)))
    return a.reshape(ndb, heads * TP, d)


def _pad_rows(a, ndb, t, rows):
    a = a.reshape(ndb, t, a.shape[-1])
    return jnp.pad(a, ((0, 0), (0, rows - t), (0, 0)))


def _unpad_o(o, t):
    return o[:, :t, :].reshape(-1, o.shape[-1])


def kernel(x_prompt, x_sample, cache_a_k, cache_a_v, cache_a_idx, cache_b_ckv, cache_b_krope, cache_d_k,
           cache_d_v, state_c_ssm, state_c_conv, page_table, norm_ffn, w_ffn_gate, w_ffn_up, w_ffn_down,
           norm_mix, w_in_even, w_out_even, mla_q_norm, mla_w_qb, mla_kv_norm, mla_w_kvb, w_in_odd,
           w_out_odd, gdn_conv_w, gdn_a_log, gdn_dt_bias, gdn_o_norm, norm_final):
    nb, seq, d = x_prompt.shape
    ndb, t_new, _ = x_sample.shape
    depth = norm_mix.shape[0]
    n_pages = page_table.shape[1]
    past = n_pages * PAGE
    n_pool = cache_a_k.shape[1]
    np_, ns_ = nb * seq, ndb * t_new
    assert seq % QB == 0 and seq % CHUNK_C == 0 and t_new <= TP and t_new >= CONV_W - 1
    n_top_p = min(TOPK_MAX, seq // 4)
    n_top_s = min(TOPK_MAX, (past + t_new) // 4)
    pp = max(p for p in (8, 4, 2, 1) if n_pages % p == 0)

    tm_p = _div_tile(math.gcd(seq, ns_), 512)
    n_ptiles = np_ // tm_p
    tiles_per_seq = seq // tm_p
    tab_index = lambda i: jnp.where(i < n_ptiles, i % tiles_per_seq, tiles_per_seq)

    pos = jnp.concatenate([jnp.arange(seq), past + (jnp.arange(tm_p) % t_new)])
    tabs = (_rope_table(pos, HD_A, ROT_A), _rope_table(pos, D_I, ROT_I), _rope_table(pos, ROPE_B, ROPE_B),
            _small_table(pos))

    wg_all, wu_all, wd_all = (w.astype(BF16) for w in (w_ffn_gate, w_ffn_up, w_ffn_down))
    g_ffn = norm_ffn.reshape(depth * 2, 1, d)
    we = w_in_even
    w_even = jnp.concatenate([we[..., 0:1024], we[..., 1064:1320], we[..., 1320:1448], we[..., 1024:1056],
                              we[..., 1448:1480], we[..., 1056:1064],
                              jnp.zeros(we.shape[:2] + (56,), we.dtype)], axis=-1).astype(BF16)
    wo = w_in_odd
    w_odd = jnp.concatenate([wo[..., 0:1536], wo[..., 1552:3088], wo[..., 1536:1552],
                             jnp.zeros(wo.shape[:2] + (112,), wo.dtype)], axis=-1).astype(BF16)
    ne = w_in_even.shape[0]
    wqb = mla_w_qb.reshape(ne, Q_LORA, H_B, NOPE_B + ROPE_B)
    wqb = jnp.concatenate([wqb[..., :NOPE_B].reshape(ne, Q_LORA, -1),
                           wqb[..., NOPE_B:].reshape(ne, Q_LORA, -1)], axis=-1).astype(BF16)
    wkvb4 = mla_w_kvb.reshape(ne, KV_LORA, H_B, NOPE_B + V_B)
    wkvb = jnp.concatenate([wkvb4[..., :NOPE_B].reshape(ne, KV_LORA, -1),
                            wkvb4[..., NOPE_B:].reshape(ne, KV_LORA, -1)], axis=-1).astype(BF16)
    wk_heads = wkvb4[..., :NOPE_B].transpose(0, 2, 1, 3).astype(BF16)
    wv_heads = wkvb4[..., NOPE_B:].transpose(0, 2, 1, 3).astype(BF16)
    w_out_e = w_out_even.astype(BF16)
    w_out_o = w_out_odd.astype(BF16)

    tri = jnp.asarray(np.triu(np.ones((128, 128), np.float32)), BF16)
    pt_flat = page_table.reshape(-1).astype(jnp.int32)
    pool_ak = cache_a_k.reshape(cache_a_k.shape[0], n_pool, PAGE, KV_A * HD_A)
    pool_av = cache_a_v.reshape(cache_a_v.shape[0], n_pool, PAGE, KV_A * HD_A)
    pool_dk = cache_d_k.reshape(cache_d_k.shape[0], n_pool, PAGE, KV_D * HD_D)
    pool_dv = cache_d_v.reshape(cache_d_v.shape[0], n_pool, PAGE, KV_D * HD_D)

    x = jnp.concatenate([x_prompt.reshape(np_, d), x_sample.reshape(ns_, d)], axis=0)
    outs = {k: [] for k in ("a_k", "a_v", "a_idx", "b_ckv", "b_krope", "d_k", "d_v", "c_ssm", "c_conv")}

    for l in range(depth):
        j = l // 2
        x = _ffn(x, g_ffn, wg_all, wu_all, wd_all, l, 0)
        g_mix = norm_mix[l].reshape(1, d)
        if l % 2 == 0:
            (qa, ka, va, kva, qi, sm, qn, qr, c, kn, vb) = _proj_even(
                x, g_mix, w_even[j], tabs, mla_q_norm[j].reshape(1, -1), wqb[j],
                mla_kv_norm[j].reshape(1, -1), wkvb[j], tm_p, tab_index)
            outs["a_k"].append(ka)
            outs["a_v"].append(va)
            outs["a_idx"].append(sm[:, 0:D_I])
            outs["b_ckv"].append(c)
            outs["b_krope"].append(sm[:, D_I:D_I + ROPE_B])
            smp = sm[:np_].reshape(nb, seq, 128)
            kit = smp[:, :, 0:D_I].transpose(0, 2, 1).astype(BF16)
            krp = smp[:, :, D_I:D_I + ROPE_B].astype(BF16)
            o_p = _attn_even_prompt(qa, qi, sm, kit, kva, qn, qr, kn, krp, vb, tri, nb, seq, n_top_p)
            sms = sm[np_:]
            qi_s = _rows_ht(qi[np_:], ndb, t_new, H_I, D_I)
            wi_s = _pad_rows(sms[:, 64:64 + H_I], ndb, t_new, TP)
            ki_new = _pad_rows(sms[:, 0:D_I], ndb, t_new, PAGE).astype(BF16)
            sp, sn = _idx_sample(pt_flat, qi_s, wi_s, ki_new, cache_a_idx, j, ndb, n_pages, pp)
            bp, bn = _select_sample(sp.reshape(ndb * TP, past), sn.reshape(ndb * TP, PAGE), tri, n_top_s, t_new)
            o_s = _attn_even_sample(
                pt_flat, _rows_ht(qa[np_:], ndb, t_new, H_A, HD_A), _rows_ht(qn[np_:], ndb, t_new, H_B, NOPE_B),
                _rows_ht(qr[np_:], ndb, t_new, H_B, ROPE_B), bp.reshape(ndb, TP, past), bn.reshape(ndb, TP, PAGE),
                _pad_rows(kva[np_:], ndb, t_new, PAGE), _pad_rows(c[np_:], ndb, t_new, PAGE).astype(BF16),
                _pad_rows(sms[:, D_I:D_I + ROPE_B], ndb, t_new, PAGE).astype(BF16), wk_heads[j], wv_heads[j],
                pool_ak, pool_av, cache_b_ckv, cache_b_krope, j, ndb, n_pages, pp, t_new)
            o = jnp.concatenate([o_p, _unpad_o(o_s, t_new)], axis=0)
            x = _outproj(x, o, o, w_out_e[j], 0, 1)
        else:
            qkv, gate, qd, kd, vd, kvd, sm = _proj_odd(x, g_mix, w_odd[j], tm_p)
            outs["d_k"].append(kd)
            outs["d_v"].append(vd)
            args = (gdn_conv_w[j], gdn_a_log[j], gdn_dt_bias[j], gdn_o_norm[j])
            zc = jnp.zeros((nb, 8, C_CONV_CH), F32)
            zs = jnp.zeros((nb, H_C, DK_C, DV_C), F32)
            oc_p, s_p, cv_p = _gdn(qkv, sm, gate, *args, zc, zs, nb, seq // CHUNK_C, CHUNK_C)
            od_p = _sb_prompt(qd, kvd, nb, seq)
            padc = lambda a: _pad_rows(a[np_:], ndb, t_new, CHUNK_C).reshape(ndb * CHUNK_C, -1)
            conv0 = jnp.pad(state_c_conv[j], ((0, 0), (8 - (CONV_W - 1), 0), (0, 0)))
            oc_s, s_s, cv_s = _gdn(padc(qkv), padc(sm), padc(gate), *args, conv0, state_c_ssm[j], ndb, 1, t_new)
            oc_s = oc_s.reshape(ndb, CHUNK_C, -1)[:, :t_new].reshape(ns_, -1)
            od_s = _sb_sample(pt_flat, _rows_ht(qd[np_:], ndb, t_new, H_D, HD_D),
                              _pad_rows(kvd[np_:], ndb, t_new, PAGE), pool_dk, pool_dv, j, ndb, n_pages, pp, t_new)
            outs["c_ssm"].append((s_p, s_s))
            outs["c_conv"].append((cv_p, cv_s))
            oc = jnp.concatenate([oc_p, oc_s], axis=0)
            od = jnp.concatenate([od_p, _unpad_o(od_s, t_new)], axis=0)
            x = _outproj(x, oc, od, w_out_o[j], 0, 0)
        x = _ffn(x, g_ffn, wg_all, wu_all, wd_all, l, 1)

    y = _final_norm(x, norm_final.reshape(1, d))

    def group(name, lo, hi, lead, tail):
        return jnp.stack([a[lo:hi].reshape(lead + tail) for a in outs[name]])

    res = [y[:np_].reshape(nb, seq, d), y[np_:].reshape(ndb, t_new, d)]
    for lo, hi, lead, gi in ((0, np_, (nb, seq), 0), (np_, np_ + ns_, (ndb, t_new), 1)):
        res += [group("a_k", lo, hi, lead, (KV_A, HD_A)), group("a_v", lo, hi, lead, (KV_A, HD_A)),
                group("a_idx", lo, hi, lead, (D_I,)), group("b_ckv", lo, hi, lead, (KV_LORA,)),
                group("b_krope", lo, hi, lead, (ROPE_B,)), group("d_k", lo, hi, lead, (KV_D, HD_D)),
                group("d_v", lo, hi, lead, (KV_D, HD_D)),
                jnp.stack([a[gi] for a in outs["c_ssm"]]), jnp.stack([a[gi] for a in outs["c_conv"]])]
    return tuple(res)
```

```python
import functools
import math

import numpy as np
import jax
import jax.numpy as jnp
from jax import lax
from jax.experimental import pallas as pl
from jax.experimental.pallas import tpu as pltpu

F32 = jnp.float32
BF16 = jnp.bfloat16
NEG_INF = float("-inf")
INT_MIN = -(2 ** 31)

EPS = 1e-6
ROPE_THETA = 500000.0
PAGE = 128
QB = 128

H_A, HD_A, KV_A, ROT_A = 8, 64, 2, 16
H_I, D_I, ROT_I = 8, 32, 8
TOPK_MAX = 256
H_B, NOPE_B, ROPE_B, V_B = 8, 64, 32, 64
Q_LORA, KV_LORA = 256, 128
MLA_SCALE = (NOPE_B + ROPE_B) ** -0.5
H_C, DK_C, DV_C, CONV_W, CHUNK_C = 8, 64, 64, 4, 64
C_CONV_CH = H_C * (2 * DK_C + DV_C)
H_D, HD_D, KV_D = 8, 64, 4
TP = 8

VMEM_LIMIT = 56 * 2 ** 20


def _cp(sem):
    return pltpu.CompilerParams(dimension_semantics=sem, vmem_limit_bytes=VMEM_LIMIT)


def _div_tile(n, cap):
    best = None
    for t in range(8, min(n, cap) + 1, 8):
        if n % t == 0:
            best = t
    assert best is not None, (n, cap)
    return best


def _rms(x, g):
    return x * lax.rsqrt(jnp.mean(x * x, axis=-1, keepdims=True) + EPS) * g


def _dot(a, b):
    return jnp.dot(a.astype(BF16), b.astype(BF16), preferred_element_type=F32)


def _dot_nt(a, b):
    return lax.dot_general(a.astype(BF16), b.astype(BF16), (((1,), (1,)), ((), ())),
                           preferred_element_type=F32)


def _dot_tn(a, b):
    return lax.dot_general(a.astype(BF16), b.astype(BF16), (((0,), (0,)), ((), ())),
                           preferred_element_type=F32)


def _split3(a):
    hi = a.astype(BF16)
    r = a - hi.astype(F32)
    mid = r.astype(BF16)
    lo = (r - mid.astype(F32)).astype(BF16)
    return hi, mid, lo


def _dot_lsplit(a, b_exact):
    hi, mid, lo = _split3(a)
    d = lambda x: jnp.dot(x, b_exact, preferred_element_type=F32)
    return d(hi) + d(mid) + d(lo)


def _dot_rsplit(a_exact, b):
    hi, mid, lo = _split3(b)
    d = lambda x: jnp.dot(a_exact, x, preferred_element_type=F32)
    return d(hi) + d(mid) + d(lo)


def _dot3(a, b):
    ah = a.astype(BF16)
    al = (a - ah.astype(F32)).astype(BF16)
    bh = b.astype(BF16)
    bl = (b - bh.astype(F32)).astype(BF16)
    d = lambda x, y: jnp.dot(x, y, preferred_element_type=F32)
    return d(ah, bh) + d(ah, bl) + d(al, bh)


def _silu(x):
    return x * jax.nn.sigmoid(x)


def _softplus(x):
    return jnp.maximum(x, 0.0) + jnp.log1p(jnp.exp(-jnp.abs(x)))


def _log_sigmoid(x):
    return jnp.minimum(x, 0.0) - jnp.log1p(jnp.exp(-jnp.abs(x)))


def _ffn_body(x_ref, g_ref, wg_ref, wu_ref, wd_ref, o_ref, n_ref, acc_ref):
    f = pl.program_id(1)

    @pl.when(f == 0)
    def _():
        n_ref[...] = _rms(x_ref[...], g_ref[...]).astype(BF16)
        acc_ref[...] = jnp.zeros_like(acc_ref)

    n = n_ref[...]
    gt = jnp.dot(n, wg_ref[...], preferred_element_type=F32)
    up = jnp.dot(n, wu_ref[...], preferred_element_type=F32)
    h = (_silu(gt) * up).astype(BF16)
    acc_ref[...] += jnp.dot(h, wd_ref[...], preferred_element_type=F32)

    @pl.when(f == pl.num_programs(1) - 1)
    def _():
        o_ref[...] = x_ref[...] + 0.5 * acc_ref[...]


def _ffn(x, g_all, wg_all, wu_all, wd_all, l, j):
    n, d = x.shape
    dff = wg_all.shape[-1]
    tm = _div_tile(n, 1280)
    tf = 256 if dff % 256 == 0 else 128
    gi = l * 2 + j
    return pl.pallas_call(
        _ffn_body,
        grid=(n // tm, dff // tf),
        in_specs=[
            pl.BlockSpec((tm, d), lambda i, f: (i, 0)),
            pl.BlockSpec((None, 1, d), lambda i, f: (gi, 0, 0)),
            pl.BlockSpec((None, None, d, tf), lambda i, f: (l, j, 0, f)),
            pl.BlockSpec((None, None, d, tf), lambda i, f: (l, j, 0, f)),
            pl.BlockSpec((None, None, tf, d), lambda i, f: (l, j, f, 0)),
        ],
        out_specs=pl.BlockSpec((tm, d), lambda i, f: (i, 0)),
        out_shape=jax.ShapeDtypeStruct((n, d), F32),
        scratch_shapes=[pltpu.VMEM((tm, d), BF16), pltpu.VMEM((tm, d), F32)],
        compiler_params=_cp(("parallel", "arbitrary")),
        name="ffn",
    )(x, g_all, wg_all, wu_all, wd_all)


def _rope_parts(pos, hd, rot):
    half = rot // 2
    inv = ROPE_THETA ** (-jnp.arange(half, dtype=F32) / half)
    ang = pos.astype(F32)[:, None] * inv[None, :]
    cos, sin = jnp.cos(ang), jnp.sin(ang)
    p = pos.shape[0]
    c = jnp.concatenate([cos, cos, jnp.ones((p, hd - rot), F32)], axis=1)
    sa = jnp.concatenate([-sin, jnp.zeros((p, hd - half), F32)], axis=1)
    sb = jnp.concatenate([jnp.zeros((p, half), F32), sin, jnp.zeros((p, hd - rot), F32)], axis=1)
    return c, sa, sb


def _rope_table(pos, hd, rot):
    return jnp.concatenate([jnp.tile(t, (1, 128 // hd)) for t in _rope_parts(pos, hd, rot)], axis=1)


def _small_table(pos):
    p = pos.shape[0]
    ci, sai, sbi = _rope_parts(pos, D_I, ROT_I)
    cb, sab, sbb = _rope_parts(pos, ROPE_B, ROPE_B)
    z = lambda w: jnp.zeros((p, w), F32)
    scale = jnp.full((p, H_I), (H_I * D_I) ** -0.5, F32)
    return jnp.concatenate([
        ci, cb, scale, z(56),
        sai, z(96), sbi, z(96),
        z(32), sab, z(64), z(32), sbb, z(64)], axis=1)


def _rope_lanes(x, tab, half):
    w = x.shape[-1]
    rep = w // 128
    parts = [tab[:, 0:128], tab[:, 128:256], tab[:, 256:384]]
    if rep > 1:
        parts = [jnp.concatenate([t] * rep, axis=1) for t in parts]
    c, sa, sb = parts
    return x * c + pltpu.roll(x, w - half, 1) * sa + pltpu.roll(x, half, 1) * sb


def _proj_even_body(x_ref, g_ref, w_ref, ta_ref, ti_ref, tb_ref, ts_ref, qng_ref, wqb_ref, kvg_ref,
                    wkvb_ref, qa_ref, ka_ref, va_ref, kva_ref, qi_ref, sm_ref, qn_ref, qr_ref, c_ref,
                    kn_ref, vb_ref):
    n = _rms(x_ref[...], g_ref[...]).astype(BF16)
    z = jnp.dot(n, w_ref[...], preferred_element_type=F32)
    ta = ta_ref[...]
    qa_ref[...] = _rope_lanes(z[:, 0:512], ta, ROT_A // 2).astype(BF16)
    ka = _rope_lanes(z[:, 512:640], ta, ROT_A // 2)
    va = z[:, 640:768]
    ka_ref[...] = ka
    va_ref[...] = va
    kva_ref[:, 0:128] = ka.astype(BF16)
    kva_ref[:, 128:256] = va.astype(BF16)
    qi_ref[...] = _rope_lanes(z[:, 768:1024], ti_ref[...], ROT_I // 2).astype(BF16)
    sm = z[:, 1408:1536]
    ts = ts_ref[...]
    sm_ref[...] = (sm * ts[:, 0:128]
                   + pltpu.roll(sm, 128 - ROT_I // 2, 1) * ts[:, 128:256]
                   + pltpu.roll(sm, ROT_I // 2, 1) * ts[:, 256:384]
                   + pltpu.roll(sm, 128 - ROPE_B // 2, 1) * ts[:, 384:512]
                   + pltpu.roll(sm, ROPE_B // 2, 1) * ts[:, 512:640])
    qln = _rms(z[:, 1024:1280], qng_ref[...]).astype(BF16)
    qb = jnp.dot(qln, wqb_ref[...], preferred_element_type=F32)
    qn_ref[...] = qb[:, 0:512].astype(BF16)
    qr_ref[...] = _rope_lanes(qb[:, 512:768], tb_ref[...], ROPE_B // 2).astype(BF16)
    c = _rms(z[:, 1280:1408], kvg_ref[...])
    c_ref[...] = c
    kvb = jnp.dot(c.astype(BF16), wkvb_ref[...], preferred_element_type=F32)
    kn_ref[...] = kvb[:, 0:512].astype(BF16)
    vb_ref[...] = kvb[:, 512:1024].astype(BF16)


def _proj_even(x, g, w, tabs, qng, wqb, kvg, wkvb, tm, tab_index):
    n, d = x.shape
    ta, ti, tb, ts = tabs
    row = lambda width: pl.BlockSpec((tm, width), lambda i: (i, 0))
    full = lambda a: pl.BlockSpec(a.shape, lambda i: (0,) * a.ndim)
    tabspec = lambda a: pl.BlockSpec((tm, a.shape[1]), lambda i: (tab_index(i), 0))
    widths = [(512, BF16), (128, F32), (128, F32), (256, BF16), (256, BF16), (128, F32), (512, BF16),
              (256, BF16), (128, F32), (512, BF16), (512, BF16)]
    return pl.pallas_call(
        _proj_even_body,
        grid=(n // tm,),
        in_specs=[row(d), full(g), full(w), tabspec(ta), tabspec(ti), tabspec(tb), tabspec(ts),
                  full(qng), full(wqb), full(kvg), full(wkvb)],
        out_specs=[row(wd) for wd, _ in widths],
        out_shape=[jax.ShapeDtypeStruct((n, wd), dt) for wd, dt in widths],
        compiler_params=_cp(("parallel",)),
        name="proj_even",
    )(x, g, w, ta, ti, tb, ts, qng, wqb, kvg, wkvb)


def _proj_odd_body(x_ref, g_ref, w_ref, qkv_ref, gate_ref, qd_ref, kd_ref, vd_ref, kvd_ref, sm_ref):
    n = _rms(x_ref[...], g_ref[...]).astype(BF16)
    z = jnp.dot(n, w_ref[...], preferred_element_type=F32)
    qkv_ref[...] = z[:, 0:1536]
    gate_ref[...] = z[:, 1536:2048]
    qd_ref[...] = z[:, 2048:2560].astype(BF16)
    kd_ref[...] = z[:, 2560:2816]
    vd_ref[...] = z[:, 2816:3072]
    kvd_ref[...] = z[:, 2560:3072].astype(BF16)
    sm_ref[...] = z[:, 3072:3200]


def _proj_odd(x, g, w, tm):
    n, d = x.shape
    row = lambda width: pl.BlockSpec((tm, width), lambda i: (i, 0))
    full = lambda a: pl.BlockSpec(a.shape, lambda i: (0,) * a.ndim)
    widths = [(1536, F32), (512, F32), (512, BF16), (256, F32), (256, F32), (512, BF16), (128, F32)]
    return pl.pallas_call(
        _proj_odd_body,
        grid=(n // tm,),
        in_specs=[row(d), full(g), full(w)],
        out_specs=[row(wd) for wd, _ in widths],
        out_shape=[jax.ShapeDtypeStruct((n, wd), dt) for wd, dt in widths],
        compiler_params=_cp(("parallel",)),
        name="proj_odd",
    )(x, g, w)


def _select_bias(sc, valid, k, key_ref, tri):
    r, w = sc.shape
    sc = jnp.where(sc == 0.0, 0.0, sc)
    bits = lax.bitcast_convert_type(sc, jnp.int32)
    key = jnp.where(bits < 0, bits ^ jnp.int32(0x7FFFFFFF), bits)
    key_ref[...] = jnp.where(valid, key, jnp.int32(INT_MIN))
    kf = float(k)

    def count_ge(t):
        return jnp.sum(jnp.where(key_ref[...] >= t, 1.0, 0.0), axis=1, keepdims=True)

    t0 = jnp.where(count_ge(jnp.zeros((r, 1), jnp.int32)) >= kf, jnp.int32(0), jnp.int32(INT_MIN))

    def body(i, t):
        cand = t | jnp.left_shift(jnp.int32(1), jnp.int32(30) - i)
        return jnp.where(count_ge(cand) >= kf, cand, t)

    t = lax.fori_loop(0, 31, body, t0)
    key = key_ref[...]
    gt = key > t
    eq = jnp.logical_and(key == t, valid)
    need = kf - jnp.sum(jnp.where(gt, 1.0, 0.0), axis=1, keepdims=True)
    carry = jnp.zeros((r, 1), F32)
    parts = []
    for c in range(w // 128):
        sl = slice(c * 128, (c + 1) * 128)
        eqc = jnp.where(eq[:, sl], 1.0, 0.0).astype(BF16)
        pre = jnp.dot(eqc, tri, preferred_element_type=F32) + carry
        carry = pre[:, 127:128]
        keep = jnp.logical_or(gt[:, sl], jnp.logical_and(eq[:, sl], pre <= need))
        parts.append(jnp.where(keep, 0.0, NEG_INF))
    return jnp.concatenate(parts, axis=1)


def _softmax_pv(s, v):
    m = jnp.max(s, axis=1, keepdims=True)
    p = jnp.exp(s - m)
    l = jnp.sum(p, axis=1, keepdims=True)
    return jnp.dot(p.astype(BF16), v, preferred_element_type=F32) / l


def _attn_even_prompt_body(qa_ref, qi_ref, smq_ref, kit_ref, kva_ref, qn_ref, qr_ref, kn_ref, kr_ref,
                           vb_ref, tri_ref, o_ref, key_ref, *, n_top, seq, kstep):
    i = pl.program_id(1)

    def compute(kw):
        rows = i * QB + lax.broadcasted_iota(jnp.int32, (QB, kw), 0)
        cols = lax.broadcasted_iota(jnp.int32, (QB, kw), 1)
        causal = cols <= rows
        kit = kit_ref[:, 0:kw]
        sc = jnp.zeros((QB, kw), F32)
        for h in range(H_I):
            d = jnp.dot(qi_ref[:, h * D_I:(h + 1) * D_I], kit, preferred_element_type=F32)
            sc = sc + smq_ref[:, 64 + h:65 + h] * jnp.maximum(d, 0.0)
        bias = _select_bias(sc, causal, n_top, key_ref.at[:, 0:kw], tri_ref[...])
        rep = H_A // KV_A
        for h in range(H_A):
            g = h // rep
            s = _dot_nt(qa_ref[:, h * HD_A:(h + 1) * HD_A], kva_ref[0:kw, g * HD_A:(g + 1) * HD_A])
            s = s * HD_A ** -0.5 + bias
            o = _softmax_pv(s, kva_ref[0:kw, 128 + g * HD_A:128 + (g + 1) * HD_A])
            o_ref[:, h * HD_A:(h + 1) * HD_A] = o.astype(BF16)
        cbias = jnp.where(causal, 0.0, NEG_INF)
        kr = kr_ref[0:kw, :]
        for h in range(H_B):
            s = (_dot_nt(qn_ref[:, h * NOPE_B:(h + 1) * NOPE_B], kn_ref[0:kw, h * NOPE_B:(h + 1) * NOPE_B])
                 + _dot_nt(qr_ref[:, h * ROPE_B:(h + 1) * ROPE_B], kr))
            s = s * MLA_SCALE + cbias
            o = _softmax_pv(s, vb_ref[0:kw, h * V_B:(h + 1) * V_B])
            o_ref[:, 512 + h * V_B:512 + (h + 1) * V_B] = o.astype(BF16)

    per = kstep // QB
    for v in range(seq // kstep):
        pl.when(i // per == v)(functools.partial(compute, (v + 1) * kstep))


def _key_step(seq):
    return max(QB, seq // 4)


def _attn_even_prompt(qa, qi, sm, kit, kva, qn, qr, kn, kr, vb, tri, nb, seq, n_top):
    nqb = seq // QB
    qrow = lambda w: pl.BlockSpec((QB, w), lambda b, i: (b * nqb + i, 0))
    krow = lambda w: pl.BlockSpec((seq, w), lambda b, i: (b, 0))
    return pl.pallas_call(
        functools.partial(_attn_even_prompt_body, n_top=n_top, seq=seq, kstep=_key_step(seq)),
        grid=(nb, nqb),
        in_specs=[qrow(512), qrow(256), qrow(128),
                  pl.BlockSpec((None, D_I, seq), lambda b, i: (b, 0, 0)),
                  krow(256), qrow(512), qrow(256), krow(512),
                  pl.BlockSpec((None, seq, ROPE_B), lambda b, i: (b, 0, 0)),
                  krow(512),
                  pl.BlockSpec((128, 128), lambda b, i: (0, 0))],
        out_specs=qrow(1024),
        out_shape=jax.ShapeDtypeStruct((nb * seq, 1024), BF16),
        scratch_shapes=[pltpu.VMEM((QB, seq), jnp.int32)],
        compiler_params=_cp(("parallel", "arbitrary")),
        name="attn_even_prompt",
    )(qa, qi, sm, kit, kva, qn, qr, kn, kr, vb, tri)


def _suffix_rest(lk, ms, cw, carry):
    w = lk.shape[1]
    parts = [None] * (w // cw)
    for c in reversed(range(w // cw)):
        lkc = lk[:, c * cw:(c + 1) * cw]
        r = _dot_lsplit(lkc, ms) + carry
        carry = r[:, 0:1] + lkc[:, 0:1]
        parts[c] = r
    rest = parts[0] if len(parts) == 1 else jnp.concatenate(parts, axis=1)
    return rest, carry


def _sb_prompt_body(q_ref, kv_ref, ms_ref, o_ref, *, seq, cw, kstep):
    i = pl.program_id(1)

    def compute(kw):
        rows = i * QB + lax.broadcasted_iota(jnp.int32, (QB, kw), 0)
        cols = lax.broadcasted_iota(jnp.int32, (QB, kw), 1)
        mask = cols < rows
        ms = ms_ref[...]
        rep = H_D // KV_D
        for h in range(H_D):
            g = h // rep
            z = _dot_nt(q_ref[:, h * HD_D:(h + 1) * HD_D], kv_ref[0:kw, g * HD_D:(g + 1) * HD_D]) * HD_D ** -0.5
            ls = _log_sigmoid(z)
            lk = jnp.where(mask, ls - z, 0.0)
            rest, _ = _suffix_rest(lk, ms, cw, jnp.zeros((QB, 1), F32))
            a = jnp.where(mask, jnp.exp(ls + rest), 0.0)
            o = jnp.dot(a.astype(BF16), kv_ref[0:kw, 256 + g * HD_D:256 + (g + 1) * HD_D],
                        preferred_element_type=F32)
            o_ref[:, h * HD_D:(h + 1) * HD_D] = o.astype(BF16)

    per = kstep // QB
    for v in range(seq // kstep):
        pl.when(i // per == v)(functools.partial(compute, (v + 1) * kstep))


def _sb_prompt(qd, kvd, nb, seq):
    nqb = seq // QB
    kstep = _key_step(seq)
    cw = 256 if kstep % 256 == 0 else 128
    ms = jnp.asarray(np.tril(np.ones((cw, cw), np.float32), -1), BF16)
    return pl.pallas_call(
        functools.partial(_sb_prompt_body, seq=seq, cw=cw, kstep=kstep),
        grid=(nb, nqb),
        in_specs=[pl.BlockSpec((QB, 512), lambda b, i: (b * nqb + i, 0)),
                  pl.BlockSpec((seq, 512), lambda b, i: (b, 0)),
                  pl.BlockSpec((cw, cw), lambda b, i: (0, 0))],
        out_specs=pl.BlockSpec((QB, 512), lambda b, i: (b * nqb + i, 0)),
        out_shape=jax.ShapeDtypeStruct((nb * seq, 512), BF16),
        compiler_params=_cp(("parallel", "arbitrary")),
        name="sb_prompt",
    )(qd, kvd, ms)


def _gdn_body(qkv_ref, sm_ref, gate_ref, cw_ref, al_ref, dtb_ref, alc_ref, dtc_ref, on_ref, bd_ref,
              tril_ref, conv0_ref, s0_ref, o_ref, sout_ref, cout_ref, xbuf, s_scr, *, valid):
    C = CHUNK_C
    c = pl.program_id(1)
    nc = pl.num_programs(1)

    @pl.when(c == 0)
    def _():
        xbuf[0:8, :] = conv0_ref[...]
        s_scr[...] = s0_ref[...]

    xbuf[8:8 + C, :] = qkv_ref[...]
    y = (cw_ref[3:4, :] * xbuf[8:8 + C, :] + cw_ref[2:3, :] * xbuf[7:7 + C, :]
         + cw_ref[1:2, :] * xbuf[6:6 + C, :] + cw_ref[0:1, :] * xbuf[5:5 + C, :])
    xc = _silu(y)

    @pl.when(c == nc - 1)
    def _():
        cout_ref[...] = xbuf[8 + valid - (CONV_W - 1):8 + valid, :]

    xbuf[0:8, :] = xbuf[C:C + 8, :]

    hk = H_C * DK_C
    q, k, v = xc[:, 0:hk], xc[:, hk:2 * hk], xc[:, 2 * hk:]
    bd = bd_ref[...]
    qn = q * lax.rsqrt(_dot_lsplit(q * q, bd) + EPS) * DK_C ** -0.5
    kn = k * lax.rsqrt(_dot_lsplit(k * k, bd) + EPS)

    sm = sm_ref[...]
    smt = sm.T
    g = -jnp.exp(al_ref[...]) * _softplus(sm + dtb_ref[...])
    gt = -jnp.exp(alc_ref[...]) * _softplus(smt[0:H_C, :] + dtc_ref[...])
    beta = jax.nn.sigmoid(sm)
    if valid < C:
        g = jnp.where(lax.broadcasted_iota(jnp.int32, g.shape, 0) < valid, g, 0.0)
        gt = jnp.where(lax.broadcasted_iota(jnp.int32, gt.shape, 1) < valid, gt, 0.0)
        beta = jnp.where(lax.broadcasted_iota(jnp.int32, beta.shape, 0) < valid, beta, 0.0)
    tril = tril_ref[...]
    gc = _dot_rsplit(tril, g)
    gct = _dot_nt_lsplit(gt, tril)

    ri = lax.broadcasted_iota(jnp.int32, (C, C), 0)
    ci = lax.broadcasted_iota(jnp.int32, (C, C), 1)
    incl = ri >= ci
    strict = ri > ci
    levels = max(1, math.ceil(math.log2(valid)))
    hs = range(H_C)
    sls = [slice(h * DK_C, (h + 1) * DK_C) for h in hs]
    gcol = [gc[:, h:h + 1] for h in hs]
    gam = [jnp.exp(jnp.where(incl, gcol[h] - gct[h:h + 1, :], NEG_INF)) for h in hs]
    kh = [kn[:, sls[h]] for h in hs]
    qh = [qn[:, sls[h]] for h in hs]
    bh = [beta[:, H_C + h:H_C + h + 1] for h in hs]
    kb = [kh[h] * bh[h] for h in hs]
    kk = [_dot_nt(jnp.concatenate([kb[h], qh[h]], axis=0), kh[h]) for h in hs]
    pw = [jnp.where(strict, -(kk[h][0:C] * gam[h]), 0.0) for h in hs]
    a_qk = [kk[h][C:2 * C] * gam[h] for h in hs]
    x = [jnp.concatenate([v[:, sls[h]] * bh[h], kb[h] * jnp.exp(gcol[h])], axis=1) for h in hs]
    for lvl in range(levels):
        if lvl < levels - 1:
            y = [_dot(pw[h], jnp.concatenate([x[h], pw[h]], axis=1)) for h in hs]
            x = [x[h] + y[h][:, 0:2 * DV_C] for h in hs]
            pw = [y[h][:, 2 * DV_C:] for h in hs]
        else:
            x = [x[h] + _dot(pw[h], x[h]) for h in hs]
    sh = [s_scr[h] for h in hs]
    ws = [_dot(jnp.concatenate([x[h][:, DV_C:], qh[h] * jnp.exp(gcol[h])], axis=0), sh[h])
          for h in hs]
    v_new = [x[h][:, 0:DV_C] - ws[h][0:C] for h in hs]
    oh = [ws[h][C:2 * C] + _dot(a_qk[h], v_new[h]) for h in hs]
    glast = [gcol[h][C - 1:C, :] for h in hs]
    k_dec = [kh[h] * jnp.exp(glast[h] - gcol[h]) for h in hs]
    s_new = [sh[h] * jnp.exp(glast[h]) + _dot_tn(k_dec[h], v_new[h]) for h in hs]
    for h in hs:
        s_scr[h] = s_new[h]
        on = _rms(oh[h], on_ref[...])
        o_ref[:, sls[h]] = (on * _silu(gate_ref[:, sls[h]])).astype(BF16)

    @pl.when(c == nc - 1)
    def _():
        sout_ref[...] = s_scr[...]


def _dot_nt_lsplit(a, b_exact):
    hi, mid, lo = _split3(a)
    d = lambda x: lax.dot_general(x, b_exact, (((1,), (1,)), ((), ())), preferred_element_type=F32)
    return d(hi) + d(mid) + d(lo)


def _gdn(qkv, sm, gate, conv_w, a_log, dt_bias, o_norm, conv0, s0, ng, nchunk, valid):
    C = CHUNK_C
    pad128 = lambda a: jnp.pad(a.reshape(1, -1), ((0, 0), (0, 128 - a.size)))
    al, dtb = pad128(a_log), pad128(dt_bias)
    alc, dtc = a_log.reshape(H_C, 1), dt_bias.reshape(H_C, 1)
    on = o_norm.reshape(1, DV_C)
    hk = H_C * DK_C
    bd = jnp.asarray(np.kron(np.eye(H_C, dtype=np.float32), np.ones((DK_C, DK_C), np.float32)), BF16)
    tril = jnp.asarray(np.tril(np.ones((C, C), np.float32)), BF16)
    row = lambda w: pl.BlockSpec((C, w), lambda b, c: (b * nchunk + c, 0))
    full = lambda a: pl.BlockSpec(a.shape, lambda b, c: (0,) * a.ndim)
    return pl.pallas_call(
        functools.partial(_gdn_body, valid=valid),
        grid=(ng, nchunk),
        in_specs=[row(3 * hk), row(128), row(hk), full(conv_w), full(al), full(dtb), full(alc), full(dtc),
                  full(on), full(bd), full(tril),
                  pl.BlockSpec((None, 8, 3 * hk), lambda b, c: (b, 0, 0)),
                  pl.BlockSpec((None, H_C, DK_C, DV_C), lambda b, c: (b, 0, 0, 0))],
        out_specs=[row(hk),
                   pl.BlockSpec((None, H_C, DK_C, DV_C), lambda b, c: (b, 0, 0, 0)),
                   pl.BlockSpec((None, CONV_W - 1, 3 * hk), lambda b, c: (b, 0, 0))],
        out_shape=[jax.ShapeDtypeStruct((ng * nchunk * C, hk), BF16),
                   jax.ShapeDtypeStruct((ng, H_C, DK_C, DV_C), F32),
                   jax.ShapeDtypeStruct((ng, CONV_W - 1, 3 * hk), F32)],
        scratch_shapes=[pltpu.VMEM((C + 8, 3 * hk), F32), pltpu.VMEM((H_C, DK_C, DV_C), F32)],
        compiler_params=_cp(("parallel", "arbitrary")),
        name="gdn",
    )(qkv, sm, gate, conv_w, al, dtb, alc, dtc, on, bd, tril, conv0, s0)


def _outproj_body(x_ref, a1_ref, a2_ref, w_ref, o_ref):
    h = a1_ref.shape[1]
    o_ref[...] = (x_ref[...] + jnp.dot(a1_ref[...], w_ref[0:h, :], preferred_element_type=F32)
                  + jnp.dot(a2_ref[...], w_ref[h:, :], preferred_element_type=F32))


def _outproj(x, a1, a2, w, c1, c2):
    n, d = x.shape
    tm = _div_tile(n, 1280)
    return pl.pallas_call(
        _outproj_body,
        grid=(n // tm,),
        in_specs=[pl.BlockSpec((tm, d), lambda i: (i, 0)),
                  pl.BlockSpec((tm, 512), lambda i: (i, c1)),
                  pl.BlockSpec((tm, 512), lambda i: (i, c2)),
                  pl.BlockSpec(w.shape, lambda i: (0, 0))],
        out_specs=pl.BlockSpec((tm, d), lambda i: (i, 0)),
        out_shape=jax.ShapeDtypeStruct((n, d), F32),
        compiler_params=_cp(("parallel",)),
        name="outproj",
    )(x, a1, a2, w)


def _final_norm_body(x_ref, g_ref, o_ref):
    o_ref[...] = _rms(x_ref[...], g_ref[...])


def _final_norm(x, g):
    n, d = x.shape
    tm = _div_tile(n, 1280)
    return pl.pallas_call(
        _final_norm_body,
        grid=(n // tm,),
        in_specs=[pl.BlockSpec((tm, d), lambda i: (i, 0)), pl.BlockSpec((1, d), lambda i: (0, 0))],
        out_specs=pl.BlockSpec((tm, d), lambda i: (i, 0)),
        out_shape=jax.ShapeDtypeStruct((n, d), F32),
        compiler_params=_cp(("parallel",)),
        name="final_norm",
    )(x, g)


def _page_specs(shape_tail, layer, n_pages, pp, nsteps, reverse=False):
    specs = []
    for p in range(pp):
        def imap(b, st, pt, p=p):
            if reverse:
                lp = n_pages - 1 - (jnp.maximum(st - 1, 0) * pp + p)
            else:
                lp = jnp.minimum(st, nsteps - 1) * pp + p
            return (layer, pt[b * n_pages + lp]) + (0,) * len(shape_tail)
        specs.append(pl.BlockSpec((None, None) + shape_tail, imap))
    return specs


def _idx_sample_body(pt_ref, qi_ref, wi_ref, knew_ref, *rest, pp):
    pages = rest[:pp]
    sp_ref, sn_ref = rest[pp], rest[pp + 1]
    st = pl.program_id(1)
    ns = pl.num_programs(1)
    qi = qi_ref[...]
    wi = wi_ref[...]

    def score(kt):
        d = _dot(qi, kt)
        acc = jnp.zeros((TP, PAGE), F32)
        for h in range(H_I):
            acc = acc + wi[:, h:h + 1] * jnp.maximum(d[h * TP:(h + 1) * TP, :], 0.0)
        return acc

    @pl.when(st < ns - 1)
    def _():
        for p in range(pp):
            sp_ref[:, p * PAGE:(p + 1) * PAGE] = score(pages[p][...])

    @pl.when(st == ns - 1)
    def _():
        sn_ref[...] = score(knew_ref[...])


def _idx_sample(pt, qi_s, wi_s, ki_new, pool_i, layer, ndb, n_pages, pp):
    nsteps = n_pages // pp
    grid_spec = pltpu.PrefetchScalarGridSpec(
        num_scalar_prefetch=1,
        grid=(ndb, nsteps + 1),
        in_specs=[pl.BlockSpec((None, H_I * TP, D_I), lambda b, st, pt: (b, 0, 0)),
                  pl.BlockSpec((None, TP, H_I), lambda b, st, pt: (b, 0, 0)),
                  pl.BlockSpec((None, D_I, PAGE), lambda b, st, pt: (b, 0, 0))]
        + _page_specs((D_I, PAGE), layer, n_pages, pp, nsteps),
        out_specs=[pl.BlockSpec((None, TP, pp * PAGE), lambda b, st, pt: (b, 0, jnp.minimum(st, nsteps - 1))),
                   pl.BlockSpec((None, TP, PAGE), lambda b, st, pt: (b, 0, 0))],
    )
    return pl.pallas_call(
        functools.partial(_idx_sample_body, pp=pp),
        grid_spec=grid_spec,
        out_shape=[jax.ShapeDtypeStruct((ndb, TP, n_pages * PAGE), F32),
                   jax.ShapeDtypeStruct((ndb, TP, PAGE), F32)],
        compiler_params=_cp(("parallel", "arbitrary")),
        name="idx_sample",
    )(pt, qi_s, wi_s, ki_new, *([pool_i] * pp))


def _select_sample_body(sp_ref, sn_ref, tri_ref, bp_ref, bn_ref, key_ref, *, n_top, past, t_new):
    sc = jnp.concatenate([sp_ref[...], sn_ref[...]], axis=1)
    r, w = sc.shape
    t = lax.broadcasted_iota(jnp.int32, (r, w), 0) & (TP - 1)
    cols = lax.broadcasted_iota(jnp.int32, (r, w), 1)
    valid = jnp.logical_or(cols < past, jnp.logical_and(cols - past <= t, cols - past < t_new))
    bias = _select_bias(sc, valid, n_top, key_ref, tri_ref[...])
    bp_ref[...] = bias[:, 0:past]
    bn_ref[...] = bias[:, past:]


def _select_sample(sp, sn, tri, n_top, t_new):
    rows, past = sp.shape
    tr = _div_tile(rows, 64)
    return pl.pallas_call(
        functools.partial(_select_sample_body, n_top=n_top, past=past, t_new=t_new),
        grid=(rows // tr,),
        in_specs=[pl.BlockSpec((tr, past), lambda i: (i, 0)),
                  pl.BlockSpec((tr, PAGE), lambda i: (i, 0)),
                  pl.BlockSpec((128, 128), lambda i: (0, 0))],
        out_specs=[pl.BlockSpec((tr, past), lambda i: (i, 0)),
                   pl.BlockSpec((tr, PAGE), lambda i: (i, 0))],
        out_shape=[jax.ShapeDtypeStruct((rows, past), F32), jax.ShapeDtypeStruct((rows, PAGE), F32)],
        scratch_shapes=[pltpu.VMEM((tr, past + PAGE), jnp.int32)],
        compiler_params=_cp(("parallel",)),
        name="select_sample",
    )(sp, sn, tri)


def _online_update(m_ref, l_ref, acc_ref, rows, s, pv):
    m_old = m_ref[rows, :]
    m_new = jnp.maximum(m_old, jnp.max(s, axis=1, keepdims=True))
    m_safe = jnp.where(m_new == NEG_INF, 0.0, m_new)
    p = jnp.exp(s - m_safe)
    alpha = jnp.exp(m_old - m_safe)
    l_ref[rows, :] = alpha * l_ref[rows, :] + jnp.sum(p, axis=1, keepdims=True)
    pb = p.astype(BF16)
    acc = alpha * acc_ref[rows, :]
    for i in range(s.shape[1] // PAGE):
        acc = acc + pv(pb[:, i * PAGE:(i + 1) * PAGE], i)
    acc_ref[rows, :] = acc
    m_ref[rows, :] = m_new


def _lane_cat(parts):
    return parts[0] if len(parts) == 1 else jnp.concatenate(parts, axis=1)


def _attn_even_sample_body(pt_ref, qa_ref, qn_ref, qr_ref, bp_ref, bn_ref, ktn_ref, vtn_ref, cn_ref, krn_ref,
                           wk_ref, wv_ref, *rest, pp, t_new):
    kpg, vpg, cpg, rpg = rest[0:pp], rest[pp:2 * pp], rest[2 * pp:3 * pp], rest[3 * pp:4 * pp]
    o_ref = rest[4 * pp]
    qlat, m_a, l_a, acc_a, m_b, l_b, acc_b = rest[4 * pp + 1:]
    st = pl.program_id(1)
    ns = pl.num_programs(1)
    rep = H_A // KV_A
    rows_g = rep * TP

    @pl.when(st == 0)
    def _():
        for h in range(H_B):
            qlat[h * TP:(h + 1) * TP, :] = _dot_nt(qn_ref[h * TP:(h + 1) * TP, :], wk_ref[h]).astype(BF16)
        m_a[...] = jnp.full_like(m_a, NEG_INF)
        m_b[...] = jnp.full_like(m_b, NEG_INF)
        l_a[...] = jnp.zeros_like(l_a)
        l_b[...] = jnp.zeros_like(l_b)
        acc_a[...] = jnp.zeros_like(acc_a)
        acc_b[...] = jnp.zeros_like(acc_b)

    def update(kt, vt, c, krt, bias8, mla_bias):
        n = len(kt)
        bias = jnp.concatenate([bias8] * rep, axis=0)
        for g in range(KV_A):
            rows = slice(g * rows_g, (g + 1) * rows_g)
            qg = qa_ref[rows, :]
            s = _lane_cat([_dot(qg, kt[i][g]) for i in range(n)]) * HD_A ** -0.5 + bias
            _online_update(m_a, l_a, acc_a, rows, s, lambda pt_, i, g=g: _dot_nt(pt_, vt[i][g]))
        ql, qr = qlat[...], qr_ref[...]
        cb = [c[i][...].astype(BF16) for i in range(n)]
        s = _lane_cat([_dot_nt(ql, cb[i]) + _dot(qr, krt[i][...]) for i in range(n)]) * MLA_SCALE
        if mla_bias is not None:
            s = s + mla_bias
        _online_update(m_b, l_b, acc_b, slice(0, H_B * TP), s,
                       lambda pt_, i: jnp.dot(pt_, cb[i], preferred_element_type=F32))

    @pl.when(st < ns - 1)
    def _():
        update(kpg, vpg, cpg, rpg, bp_ref[...], None)

    @pl.when(st == ns - 1)
    def _():
        t = lax.broadcasted_iota(jnp.int32, (H_B * TP, PAGE), 0) & (TP - 1)
        j = lax.broadcasted_iota(jnp.int32, (H_B * TP, PAGE), 1)
        ok = jnp.logical_and(j <= t, j < t_new)
        update([ktn_ref], [vtn_ref], [cn_ref], [krn_ref], bn_ref[...], jnp.where(ok, 0.0, NEG_INF))
        oa = acc_a[...] / l_a[...]
        ol = (acc_b[...] / l_b[...]).astype(BF16)
        for h in range(H_A):
            o_ref[:, h * HD_A:(h + 1) * HD_A] = oa[h * TP:(h + 1) * TP, :].astype(BF16)
        for h in range(H_B):
            ob = jnp.dot(ol[h * TP:(h + 1) * TP, :], wv_ref[h], preferred_element_type=F32)
            o_ref[:, 512 + h * V_B:512 + (h + 1) * V_B] = ob.astype(BF16)


def _attn_even_sample(pt, qa_s, qn_s, qr_s, bias_p, bias_n, kt_new, vt_new, c_new, krt_new, wk, wv,
                      pool_kt, pool_vt, pool_c, pool_rt, layer, ndb, n_pages, pp, t_new):
    nsteps = n_pages // pp
    seqn = lambda a: pl.BlockSpec((None,) + a.shape[1:], lambda b, st, pt: (b,) + (0,) * (a.ndim - 1))
    full3 = lambda a: pl.BlockSpec(a.shape, lambda b, st, pt: (0, 0, 0))
    grid_spec = pltpu.PrefetchScalarGridSpec(
        num_scalar_prefetch=1,
        grid=(ndb, nsteps + 1),
        in_specs=[seqn(qa_s), seqn(qn_s), seqn(qr_s),
                  pl.BlockSpec((None, TP, pp * PAGE), lambda b, st, pt: (b, 0, jnp.minimum(st, nsteps - 1))),
                  seqn(bias_n), seqn(kt_new), seqn(vt_new), seqn(c_new), seqn(krt_new), full3(wk), full3(wv)]
        + _page_specs((KV_A, HD_A, PAGE), layer, n_pages, pp, nsteps)
        + _page_specs((KV_A, HD_A, PAGE), layer, n_pages, pp, nsteps)
        + _page_specs((PAGE, KV_LORA), layer, n_pages, pp, nsteps)
        + _page_specs((ROPE_B, PAGE), layer, n_pages, pp, nsteps),
        out_specs=pl.BlockSpec((None, TP, 1024), lambda b, st, pt: (b, 0, 0)),
        scratch_shapes=[pltpu.VMEM((H_B * TP, KV_LORA), BF16),
                        pltpu.VMEM((H_A * TP, 1), F32), pltpu.VMEM((H_A * TP, 1), F32),
                        pltpu.VMEM((H_A * TP, HD_A), F32),
                        pltpu.VMEM((H_B * TP, 1), F32), pltpu.VMEM((H_B * TP, 1), F32),
                        pltpu.VMEM((H_B * TP, KV_LORA), F32)],
    )
    return pl.pallas_call(
        functools.partial(_attn_even_sample_body, pp=pp, t_new=t_new),
        grid_spec=grid_spec,
        out_shape=jax.ShapeDtypeStruct((ndb, TP, 1024), BF16),
        compiler_params=_cp(("parallel", "arbitrary")),
        name="attn_even_sample",
    )(pt, qa_s, qn_s, qr_s, bias_p, bias_n, kt_new, vt_new, c_new, krt_new, wk, wv,
      *([pool_kt] * pp), *([pool_vt] * pp), *([pool_c] * pp), *([pool_rt] * pp))


def _sb_sample_body(pt_ref, q_ref, ktn_ref, vtn_ref, ms_ref, *rest, pp, t_new):
    kpg, vpg = rest[0:pp], rest[pp:2 * pp]
    o_ref = rest[2 * pp]
    carry, acc = rest[2 * pp + 1:]
    st = pl.program_id(1)
    ns = pl.num_programs(1)
    rep = H_D // KV_D
    rows_g = rep * TP
    nq = H_D * TP
    ms = ms_ref[...]

    def update(kt, vt, mask):
        n = len(kt)
        z = jnp.concatenate(
            [_dot(q_ref[g * rows_g:(g + 1) * rows_g, :], kt[i][g]) for i in range(n) for g in range(KV_D)],
            axis=0) * HD_D ** -0.5
        ls = _log_sigmoid(z)
        lk = ls - z
        if mask is not None:
            lk = jnp.where(mask, lk, 0.0)
        within = _dot_lsplit(lk, ms)
        tot = within[:, 0:1] + lk[:, 0:1]
        c = carry[...]
        rest_ = []
        for i in range(n):
            rest_.append(within[i * nq:(i + 1) * nq, :] + c)
            c = c + tot[i * nq:(i + 1) * nq, :]
        carry[...] = c
        a = jnp.exp(ls + (rest_[0] if n == 1 else jnp.concatenate(rest_, axis=0)))
        if mask is not None:
            a = jnp.where(mask, a, 0.0)
        ab = a.astype(BF16)
        for g in range(KV_D):
            o = acc[g * rows_g:(g + 1) * rows_g, :]
            for i in range(n):
                o = o + _dot_nt(ab[i * nq + g * rows_g:i * nq + (g + 1) * rows_g, :], vt[i][g])
            acc[g * rows_g:(g + 1) * rows_g, :] = o

    @pl.when(st == 0)
    def _():
        carry[...] = jnp.zeros_like(carry)
        acc[...] = jnp.zeros_like(acc)
        t = lax.broadcasted_iota(jnp.int32, (nq, PAGE), 0) & (TP - 1)
        j = lax.broadcasted_iota(jnp.int32, (nq, PAGE), 1)
        update([ktn_ref], [vtn_ref], jnp.logical_and(j < t, j < t_new))

    @pl.when(st > 0)
    def _():
        update(kpg, vpg, None)

    @pl.when(st == ns - 1)
    def _():
        a = acc[...]
        for h in range(H_D):
            o_ref[:, h * HD_D:(h + 1) * HD_D] = a[h * TP:(h + 1) * TP, :].astype(BF16)


def _sb_sample(pt, qd_s, kt_new, vt_new, pool_kt, pool_vt, layer, ndb, n_pages, pp, t_new):
    nsteps = n_pages // pp
    ms = jnp.asarray(np.tril(np.ones((PAGE, PAGE), np.float32), -1), BF16)
    seqn = lambda a: pl.BlockSpec((None,) + a.shape[1:], lambda b, st, pt: (b,) + (0,) * (a.ndim - 1))
    grid_spec = pltpu.PrefetchScalarGridSpec(
        num_scalar_prefetch=1,
        grid=(ndb, nsteps + 1),
        in_specs=[seqn(qd_s), seqn(kt_new), seqn(vt_new), pl.BlockSpec((PAGE, PAGE), lambda b, st, pt: (0, 0))]
        + _page_specs((KV_D, HD_D, PAGE), layer, n_pages, pp, nsteps, reverse=True)
        + _page_specs((KV_D, HD_D, PAGE), layer, n_pages, pp, nsteps, reverse=True),
        out_specs=pl.BlockSpec((None, TP, 512), lambda b, st, pt: (b, 0, 0)),
        scratch_shapes=[pltpu.VMEM((H_D * TP, 1), F32), pltpu.VMEM((H_D * TP, HD_D), F32)],
    )
    return pl.pallas_call(
        functools.partial(_sb_sample_body, pp=pp, t_new=t_new),
        grid_spec=grid_spec,
        out_shape=jax.ShapeDtypeStruct((ndb, TP, 512), BF16),
        compiler_params=_cp(("parallel", "arbitrary")),
        name="sb_sample",
    )(pt, qd_s, kt_new, vt_new, ms, *([pool_kt] * pp), *([pool_vt] * pp))


def _rows_ht(a, ndb, t, heads, d):
    a = a.reshape(ndb, t, heads, d).transpose(0, 2, 1, 3)
    a = jnp.pad(a, ((0, 0), (0, 0), (0, TP - t), (0, 0)))
    return a.reshape(ndb, heads * TP, d)


def _pad_rows(a, ndb, t, rows):
    a = a.reshape(ndb, t, a.shape[-1])
    return jnp.pad(a, ((0, 0), (0, rows - t), (0, 0)))


def _new_t(a, ndb, t, heads, d):
    a = a.reshape(ndb, t, heads, d).transpose(0, 2, 3, 1)
    return jnp.pad(a, ((0, 0), (0, 0), (0, 0), (0, PAGE - t))).astype(BF16)


def _unpad_o(o, t):
    return o[:, :t, :].reshape(-1, o.shape[-1])


def kernel(x_prompt, x_sample, cache_a_k, cache_a_v, cache_a_idx, cache_b_ckv, cache_b_krope, cache_d_k,
           cache_d_v, state_c_ssm, state_c_conv, page_table, norm_ffn, w_ffn_gate, w_ffn_up, w_ffn_down,
           norm_mix, w_in_even, w_out_even, mla_q_norm, mla_w_qb, mla_kv_norm, mla_w_kvb, w_in_odd,
           w_out_odd, gdn_conv_w, gdn_a_log, gdn_dt_bias, gdn_o_norm, norm_final):
    nb, seq, d = x_prompt.shape
    ndb, t_new, _ = x_sample.shape
    depth = norm_mix.shape[0]
    n_pages = page_table.shape[1]
    past = n_pages * PAGE
    n_pool = cache_a_k.shape[1]
    np_, ns_ = nb * seq, ndb * t_new
    assert seq % QB == 0 and seq % CHUNK_C == 0 and t_new <= TP and t_new >= CONV_W - 1
    n_top_p = min(TOPK_MAX, seq // 4)
    n_top_s = min(TOPK_MAX, (past + t_new) // 4)
    pp = max(p for p in (16, 8, 4, 2, 1) if n_pages % p == 0)

    tm_p = _div_tile(math.gcd(seq, ns_), 512)
    n_ptiles = np_ // tm_p
    tiles_per_seq = seq // tm_p
    tab_index = lambda i: jnp.where(i < n_ptiles, i % tiles_per_seq, tiles_per_seq)

    pos = jnp.concatenate([jnp.arange(seq), past + (jnp.arange(tm_p) % t_new)])
    tabs = (_rope_table(pos, HD_A, ROT_A), _rope_table(pos, D_I, ROT_I), _rope_table(pos, ROPE_B, ROPE_B),
            _small_table(pos))

    wg_all, wu_all, wd_all = (w.astype(BF16) for w in (w_ffn_gate, w_ffn_up, w_ffn_down))
    g_ffn = norm_ffn.reshape(depth * 2, 1, d)
    we = w_in_even
    w_even = jnp.concatenate([we[..., 0:1024], we[..., 1064:1320], we[..., 1320:1448], we[..., 1024:1056],
                              we[..., 1448:1480], we[..., 1056:1064],
                              jnp.zeros(we.shape[:2] + (56,), we.dtype)], axis=-1).astype(BF16)
    wo = w_in_odd
    w_odd = jnp.concatenate([wo[..., 0:1536], wo[..., 1552:3088], wo[..., 1536:1552],
                             jnp.zeros(wo.shape[:2] + (112,), wo.dtype)], axis=-1).astype(BF16)
    ne = w_in_even.shape[0]
    wqb = mla_w_qb.reshape(ne, Q_LORA, H_B, NOPE_B + ROPE_B)
    wqb = jnp.concatenate([wqb[..., :NOPE_B].reshape(ne, Q_LORA, -1),
                           wqb[..., NOPE_B:].reshape(ne, Q_LORA, -1)], axis=-1).astype(BF16)
    wkvb4 = mla_w_kvb.reshape(ne, KV_LORA, H_B, NOPE_B + V_B)
    wkvb = jnp.concatenate([wkvb4[..., :NOPE_B].reshape(ne, KV_LORA, -1),
                            wkvb4[..., NOPE_B:].reshape(ne, KV_LORA, -1)], axis=-1).astype(BF16)
    wk_heads = wkvb4[..., :NOPE_B].transpose(0, 2, 1, 3).astype(BF16)
    wv_heads = wkvb4[..., NOPE_B:].transpose(0, 2, 1, 3).astype(BF16)
    w_out_e = w_out_even.astype(BF16)
    w_out_o = w_out_odd.astype(BF16)

    tri = jnp.asarray(np.triu(np.ones((128, 128), np.float32)), BF16)
    pt_flat = page_table.reshape(-1).astype(jnp.int32)
    pool_akt = cache_a_k.transpose(0, 1, 3, 4, 2)
    pool_avt = cache_a_v.transpose(0, 1, 3, 4, 2)
    pool_ait = cache_a_idx.transpose(0, 1, 3, 2)
    pool_brt = cache_b_krope.transpose(0, 1, 3, 2)
    pool_dkt = cache_d_k.transpose(0, 1, 3, 4, 2)
    pool_dvt = cache_d_v.transpose(0, 1, 3, 4, 2)

    x = jnp.concatenate([x_prompt.reshape(np_, d), x_sample.reshape(ns_, d)], axis=0)
    outs = {k: [] for k in ("a_k", "a_v", "a_idx", "b_ckv", "b_krope", "d_k", "d_v", "c_ssm", "c_conv")}

    for l in range(depth):
        j = l // 2
        x = _ffn(x, g_ffn, wg_all, wu_all, wd_all, l, 0)
        g_mix = norm_mix[l].reshape(1, d)
        if l % 2 == 0:
            (qa, ka, va, kva, qi, sm, qn, qr, c, kn, vb) = _proj_even(
                x, g_mix, w_even[j], tabs, mla_q_norm[j].reshape(1, -1), wqb[j],
                mla_kv_norm[j].reshape(1, -1), wkvb[j], tm_p, tab_index)
            outs["a_k"].append(ka)
            outs["a_v"].append(va)
            outs["a_idx"].append(sm[:, 0:D_I])
            outs["b_ckv"].append(c)
            outs["b_krope"].append(sm[:, D_I:D_I + ROPE_B])
            smp = sm[:np_].reshape(nb, seq, 128)
            kit = smp[:, :, 0:D_I].transpose(0, 2, 1).astype(BF16)
            krp = smp[:, :, D_I:D_I + ROPE_B].astype(BF16)
            o_p = _attn_even_prompt(qa, qi, sm, kit, kva, qn, qr, kn, krp, vb, tri, nb, seq, n_top_p)
            sms = sm[np_:]
            qi_s = _rows_ht(qi[np_:], ndb, t_new, H_I, D_I)
            wi_s = _pad_rows(sms[:, 64:64 + H_I], ndb, t_new, TP)
            ki_new = _new_t(sms[:, 0:D_I], ndb, t_new, 1, D_I).reshape(ndb, D_I, PAGE)
            sp, sn = _idx_sample(pt_flat, qi_s, wi_s, ki_new, pool_ait, j, ndb, n_pages, pp)
            bp, bn = _select_sample(sp.reshape(ndb * TP, past), sn.reshape(ndb * TP, PAGE), tri, n_top_s, t_new)
            o_s = _attn_even_sample(
                pt_flat, _rows_ht(qa[np_:], ndb, t_new, H_A, HD_A), _rows_ht(qn[np_:], ndb, t_new, H_B, NOPE_B),
                _rows_ht(qr[np_:], ndb, t_new, H_B, ROPE_B), bp.reshape(ndb, TP, past), bn.reshape(ndb, TP, PAGE),
                _new_t(ka[np_:], ndb, t_new, KV_A, HD_A), _new_t(va[np_:], ndb, t_new, KV_A, HD_A),
                _pad_rows(c[np_:], ndb, t_new, PAGE).astype(BF16),
                _new_t(sms[:, D_I:D_I + ROPE_B], ndb, t_new, 1, ROPE_B).reshape(ndb, ROPE_B, PAGE),
                wk_heads[j], wv_heads[j], pool_akt, pool_avt, cache_b_ckv, pool_brt, j, ndb, n_pages, pp, t_new)
            o = jnp.concatenate([o_p, _unpad_o(o_s, t_new)], axis=0)
            x = _outproj(x, o, o, w_out_e[j], 0, 1)
        else:
            qkv, gate, qd, kd, vd, kvd, sm = _proj_odd(x, g_mix, w_odd[j], tm_p)
            outs["d_k"].append(kd)
            outs["d_v"].append(vd)
            args = (gdn_conv_w[j], gdn_a_log[j], gdn_dt_bias[j], gdn_o_norm[j])
            zc = jnp.zeros((nb, 8, C_CONV_CH), F32)
            zs = jnp.zeros((nb, H_C, DK_C, DV_C), F32)
            oc_p, s_p, cv_p = _gdn(qkv, sm, gate, *args, zc, zs, nb, seq // CHUNK_C, CHUNK_C)
            od_p = _sb_prompt(qd, kvd, nb, seq)
            padc = lambda a: _pad_rows(a[np_:], ndb, t_new, CHUNK_C).reshape(ndb * CHUNK_C, -1)
            conv0 = jnp.pad(state_c_conv[j], ((0, 0), (8 - (CONV_W - 1), 0), (0, 0)))
            oc_s, s_s, cv_s = _gdn(padc(qkv), padc(sm), padc(gate), *args, conv0, state_c_ssm[j], ndb, 1, t_new)
            oc_s = oc_s.reshape(ndb, CHUNK_C, -1)[:, :t_new].reshape(ns_, -1)
            od_s = _sb_sample(pt_flat, _rows_ht(qd[np_:], ndb, t_new, H_D, HD_D),
                              _new_t(kd[np_:], ndb, t_new, KV_D, HD_D), _new_t(vd[np_:], ndb, t_new, KV_D, HD_D),
                              pool_dkt, pool_dvt, j, ndb, n_pages, pp, t_new)
            outs["c_ssm"].append((s_p, s_s))
            outs["c_conv"].append((cv_p, cv_s))
            oc = jnp.concatenate([oc_p, oc_s], axis=0)
            od = jnp.concatenate([od_p, _unpad_o(od_s, t_new)], axis=0)
            x = _outproj(x, oc, od, w_out_o[j], 0, 0)
        x = _ffn(x, g_ffn, wg_all, wu_all, wd_all, l, 1)

    y = _final_norm(x, norm_final.reshape(1, d))

    def group(name, lo, hi, lead, tail):
        return jnp.stack([a[lo:hi].reshape(lead + tail) for a in outs[name]])

    res = [y[:np_].reshape(nb, seq, d), y[np_:].reshape(ndb, t_new, d)]
    for lo, hi, lead, gi in ((0, np_, (nb, seq), 0), (np_, np_ + ns_, (ndb, t_new), 1)):
        res += [group("a_k", lo, hi, lead, (KV_A, HD_A)), group("a_v", lo, hi, lead, (KV_A, HD_A)),
                group("a_idx", lo, hi, lead, (D_I,)), group("b_ckv", lo, hi, lead, (KV_LORA,)),
                group("b_krope", lo, hi, lead, (ROPE_B,)), group("d_k", lo, hi, lead, (KV_D, HD_D)),
                group("d_v", lo, hi, lead, (KV_D, HD_D)),
                jnp.stack([a[gi] for a in outs["c_ssm"]]), jnp.stack([a[gi] for a in outs["c_conv"]])]
    return tuple(res)
```

```python
import functools
import math

import numpy as np
import jax
import jax.numpy as jnp
from jax import lax
from jax.experimental import pallas as pl
from jax.experimental.pallas import tpu as pltpu

F32 = jnp.float32
BF16 = jnp.bfloat16
NEG_INF = float("-inf")
INT_MIN = -(2 ** 31)

EPS = 1e-6
ROPE_THETA = 500000.0
PAGE = 128
QB = 128

H_A, HD_A, KV_A, ROT_A = 8, 64, 2, 16
H_I, D_I, ROT_I = 8, 32, 8
TOPK_MAX = 256
H_B, NOPE_B, ROPE_B, V_B = 8, 64, 32, 64
Q_LORA, KV_LORA = 256, 128
MLA_SCALE = (NOPE_B + ROPE_B) ** -0.5
H_C, DK_C, DV_C, CONV_W, CHUNK_C = 8, 64, 64, 4, 64
C_CONV_CH = H_C * (2 * DK_C + DV_C)
H_D, HD_D, KV_D = 8, 64, 4
LOG2E = math.log2(math.e)
QSCALE_A = HD_A ** -0.5 * LOG2E
QSCALE_B = MLA_SCALE * LOG2E
QSCALE_D = HD_D ** -0.5
TP = 8

VMEM_LIMIT = 56 * 2 ** 20


def _cp(sem):
    return pltpu.CompilerParams(dimension_semantics=sem, vmem_limit_bytes=VMEM_LIMIT)


def _div_tile(n, cap):
    best = None
    for t in range(8, min(n, cap) + 1, 8):
        if n % t == 0:
            best = t
    assert best is not None, (n, cap)
    return best


def _rms(x, g):
    return x * lax.rsqrt(jnp.mean(x * x, axis=-1, keepdims=True) + EPS) * g


def _dot(a, b):
    return jnp.dot(a.astype(BF16), b.astype(BF16), preferred_element_type=F32)


def _dot_nt(a, b):
    return lax.dot_general(a.astype(BF16), b.astype(BF16), (((1,), (1,)), ((), ())),
                           preferred_element_type=F32)


def _dot_tn(a, b):
    return lax.dot_general(a.astype(BF16), b.astype(BF16), (((0,), (0,)), ((), ())),
                           preferred_element_type=F32)


def _split3(a):
    hi = a.astype(BF16)
    r = a - hi.astype(F32)
    mid = r.astype(BF16)
    lo = (r - mid.astype(F32)).astype(BF16)
    return hi, mid, lo


def _dot_lsplit(a, b_exact):
    hi, mid, lo = _split3(a)
    d = lambda x: jnp.dot(x, b_exact, preferred_element_type=F32)
    return d(hi) + d(mid) + d(lo)


def _dot_lsplit2(a, b_exact):
    hi = a.astype(BF16)
    lo = (a - hi.astype(F32)).astype(BF16)
    d = lambda x: jnp.dot(x, b_exact, preferred_element_type=F32)
    return d(hi) + d(lo)


def _dot_rsplit(a_exact, b):
    hi, mid, lo = _split3(b)
    d = lambda x: jnp.dot(a_exact, x, preferred_element_type=F32)
    return d(hi) + d(mid) + d(lo)


def _dot3(a, b):
    ah = a.astype(BF16)
    al = (a - ah.astype(F32)).astype(BF16)
    bh = b.astype(BF16)
    bl = (b - bh.astype(F32)).astype(BF16)
    d = lambda x, y: jnp.dot(x, y, preferred_element_type=F32)
    return d(ah, bh) + d(ah, bl) + d(al, bh)


def _silu(x):
    return x * jax.nn.sigmoid(x)


def _softplus(x):
    return jnp.maximum(x, 0.0) + jnp.log1p(jnp.exp(-jnp.abs(x)))


def _log_sigmoid(x):
    return jnp.minimum(x, 0.0) - jnp.log1p(jnp.exp(-jnp.abs(x)))


def _ffn_body(x_ref, g_ref, wg_ref, wu_ref, wd_ref, o_ref, n_ref, acc_ref):
    f = pl.program_id(1)

    @pl.when(f == 0)
    def _():
        n_ref[...] = _rms(x_ref[...], g_ref[...]).astype(BF16)
        acc_ref[...] = jnp.zeros_like(acc_ref)

    n = n_ref[...]
    gt = jnp.dot(n, wg_ref[...], preferred_element_type=F32)
    up = jnp.dot(n, wu_ref[...], preferred_element_type=F32)
    h = (_silu(gt) * up).astype(BF16)
    acc_ref[...] += jnp.dot(h, wd_ref[...], preferred_element_type=F32)

    @pl.when(f == pl.num_programs(1) - 1)
    def _():
        o_ref[...] = x_ref[...] + 0.5 * acc_ref[...]


def _ffn(x, g_all, wg_all, wu_all, wd_all, l, j):
    n, d = x.shape
    dff = wg_all.shape[-1]
    tm = _div_tile(n, 1280)
    tf = 256 if dff % 256 == 0 else 128
    gi = l * 2 + j
    return pl.pallas_call(
        _ffn_body,
        grid=(n // tm, dff // tf),
        in_specs=[
            pl.BlockSpec((tm, d), lambda i, f: (i, 0)),
            pl.BlockSpec((None, 1, d), lambda i, f: (gi, 0, 0)),
            pl.BlockSpec((None, None, d, tf), lambda i, f: (l, j, 0, f)),
            pl.BlockSpec((None, None, d, tf), lambda i, f: (l, j, 0, f)),
            pl.BlockSpec((None, None, tf, d), lambda i, f: (l, j, f, 0)),
        ],
        out_specs=pl.BlockSpec((tm, d), lambda i, f: (i, 0)),
        out_shape=jax.ShapeDtypeStruct((n, d), F32),
        scratch_shapes=[pltpu.VMEM((tm, d), BF16), pltpu.VMEM((tm, d), F32)],
        compiler_params=_cp(("parallel", "arbitrary")),
        name="ffn",
    )(x, g_all, wg_all, wu_all, wd_all)


def _rope_parts(pos, hd, rot):
    half = rot // 2
    inv = ROPE_THETA ** (-jnp.arange(half, dtype=F32) / half)
    ang = pos.astype(F32)[:, None] * inv[None, :]
    cos, sin = jnp.cos(ang), jnp.sin(ang)
    p = pos.shape[0]
    c = jnp.concatenate([cos, cos, jnp.ones((p, hd - rot), F32)], axis=1)
    sa = jnp.concatenate([-sin, jnp.zeros((p, hd - half), F32)], axis=1)
    sb = jnp.concatenate([jnp.zeros((p, half), F32), sin, jnp.zeros((p, hd - rot), F32)], axis=1)
    return c, sa, sb


def _rope_table(pos, hd, rot):
    return jnp.concatenate([jnp.tile(t, (1, 128 // hd)) for t in _rope_parts(pos, hd, rot)], axis=1)


def _small_table(pos):
    p = pos.shape[0]
    ci, sai, sbi = _rope_parts(pos, D_I, ROT_I)
    cb, sab, sbb = _rope_parts(pos, ROPE_B, ROPE_B)
    z = lambda w: jnp.zeros((p, w), F32)
    scale = jnp.full((p, H_I), (H_I * D_I) ** -0.5, F32)
    return jnp.concatenate([
        ci, cb, scale, z(56),
        sai, z(96), sbi, z(96),
        z(32), sab, z(64), z(32), sbb, z(64)], axis=1)


def _rope_lanes(x, tab, half):
    w = x.shape[-1]
    rep = w // 128
    parts = [tab[:, 0:128], tab[:, 128:256], tab[:, 256:384]]
    if rep > 1:
        parts = [jnp.concatenate([t] * rep, axis=1) for t in parts]
    c, sa, sb = parts
    return x * c + pltpu.roll(x, w - half, 1) * sa + pltpu.roll(x, half, 1) * sb


def _proj_even_body(x_ref, g_ref, w_ref, ta_ref, ti_ref, tb_ref, ts_ref, qng_ref, wqb_ref, kvg_ref,
                    wkvb_ref, qa_ref, ka_ref, va_ref, kva_ref, qi_ref, sm_ref, qn_ref, qr_ref, c_ref,
                    kn_ref, vb_ref):
    n = _rms(x_ref[...], g_ref[...]).astype(BF16)
    z = jnp.dot(n, w_ref[...], preferred_element_type=F32)
    ta = ta_ref[...]
    qa_ref[...] = (_rope_lanes(z[:, 0:512], ta, ROT_A // 2) * QSCALE_A).astype(BF16)
    ka = _rope_lanes(z[:, 512:640], ta, ROT_A // 2)
    va = z[:, 640:768]
    ka_ref[...] = ka
    va_ref[...] = va
    kva_ref[:, 0:128] = ka.astype(BF16)
    kva_ref[:, 128:256] = va.astype(BF16)
    qi_ref[...] = _rope_lanes(z[:, 768:1024], ti_ref[...], ROT_I // 2).astype(BF16)
    sm = z[:, 1408:1536]
    ts = ts_ref[...]
    sm_ref[...] = (sm * ts[:, 0:128]
                   + pltpu.roll(sm, 128 - ROT_I // 2, 1) * ts[:, 128:256]
                   + pltpu.roll(sm, ROT_I // 2, 1) * ts[:, 256:384]
                   + pltpu.roll(sm, 128 - ROPE_B // 2, 1) * ts[:, 384:512]
                   + pltpu.roll(sm, ROPE_B // 2, 1) * ts[:, 512:640])
    qln = _rms(z[:, 1024:1280], qng_ref[...]).astype(BF16)
    qb = jnp.dot(qln, wqb_ref[...], preferred_element_type=F32)
    qn_ref[...] = (qb[:, 0:512] * QSCALE_B).astype(BF16)
    qr_ref[...] = (_rope_lanes(qb[:, 512:768], tb_ref[...], ROPE_B // 2) * QSCALE_B).astype(BF16)
    c = _rms(z[:, 1280:1408], kvg_ref[...])
    c_ref[...] = c
    kvb = jnp.dot(c.astype(BF16), wkvb_ref[...], preferred_element_type=F32)
    kn_ref[...] = kvb[:, 0:512].astype(BF16)
    vb_ref[...] = kvb[:, 512:1024].astype(BF16)


def _proj_even(x, g, w, tabs, qng, wqb, kvg, wkvb, tm, tab_index):
    n, d = x.shape
    ta, ti, tb, ts = tabs
    row = lambda width: pl.BlockSpec((tm, width), lambda i: (i, 0))
    full = lambda a: pl.BlockSpec(a.shape, lambda i: (0,) * a.ndim)
    tabspec = lambda a: pl.BlockSpec((tm, a.shape[1]), lambda i: (tab_index(i), 0))
    widths = [(512, BF16), (128, F32), (128, F32), (256, BF16), (256, BF16), (128, F32), (512, BF16),
              (256, BF16), (128, F32), (512, BF16), (512, BF16)]
    return pl.pallas_call(
        _proj_even_body,
        grid=(n // tm,),
        in_specs=[row(d), full(g), full(w), tabspec(ta), tabspec(ti), tabspec(tb), tabspec(ts),
                  full(qng), full(wqb), full(kvg), full(wkvb)],
        out_specs=[row(wd) for wd, _ in widths],
        out_shape=[jax.ShapeDtypeStruct((n, wd), dt) for wd, dt in widths],
        compiler_params=_cp(("parallel",)),
        name="proj_even",
    )(x, g, w, ta, ti, tb, ts, qng, wqb, kvg, wkvb)


def _proj_odd_body(x_ref, g_ref, w_ref, qkv_ref, gate_ref, qd_ref, kd_ref, vd_ref, kvd_ref, sm_ref):
    n = _rms(x_ref[...], g_ref[...]).astype(BF16)
    z = jnp.dot(n, w_ref[...], preferred_element_type=F32)
    qkv_ref[...] = z[:, 0:1536]
    gate_ref[...] = z[:, 1536:2048]
    qd_ref[...] = (z[:, 2048:2560] * QSCALE_D).astype(BF16)
    kd_ref[...] = z[:, 2560:2816]
    vd_ref[...] = z[:, 2816:3072]
    kvd_ref[...] = z[:, 2560:3072].astype(BF16)
    sm_ref[...] = z[:, 3072:3200]


def _proj_odd(x, g, w, tm):
    n, d = x.shape
    row = lambda width: pl.BlockSpec((tm, width), lambda i: (i, 0))
    full = lambda a: pl.BlockSpec(a.shape, lambda i: (0,) * a.ndim)
    widths = [(1536, F32), (512, F32), (512, BF16), (256, F32), (256, F32), (512, BF16), (128, F32)]
    return pl.pallas_call(
        _proj_odd_body,
        grid=(n // tm,),
        in_specs=[row(d), full(g), full(w)],
        out_specs=[row(wd) for wd, _ in widths],
        out_shape=[jax.ShapeDtypeStruct((n, wd), dt) for wd, dt in widths],
        compiler_params=_cp(("parallel",)),
        name="proj_odd",
    )(x, g, w)


def _select_bias(sc, valid, k, key_ref, tri):
    r, w = sc.shape
    sc = jnp.where(sc == 0.0, 0.0, sc)
    bits = lax.bitcast_convert_type(sc, jnp.int32)
    key = jnp.where(bits < 0, bits ^ jnp.int32(0x7FFFFFFF), bits)
    key_ref[...] = jnp.where(valid, key, jnp.int32(INT_MIN))
    kf = float(k)

    def count_ge(t):
        return jnp.sum(jnp.where(key_ref[...] >= t, 1.0, 0.0), axis=1, keepdims=True)

    t0 = jnp.where(count_ge(jnp.zeros((r, 1), jnp.int32)) >= kf, jnp.int32(0), jnp.int32(INT_MIN))

    def body(i, t):
        cand = t | jnp.left_shift(jnp.int32(1), jnp.int32(30) - i)
        return jnp.where(count_ge(cand) >= kf, cand, t)

    t = lax.fori_loop(0, 31, body, t0)
    key = key_ref[...]
    gt = key > t
    eq = jnp.logical_and(key == t, valid)
    need = kf - jnp.sum(jnp.where(gt, 1.0, 0.0), axis=1, keepdims=True)
    carry = jnp.zeros((r, 1), F32)
    parts = []
    for c in range(w // 128):
        sl = slice(c * 128, (c + 1) * 128)
        eqc = jnp.where(eq[:, sl], 1.0, 0.0).astype(BF16)
        pre = jnp.dot(eqc, tri, preferred_element_type=F32) + carry
        carry = pre[:, 127:128]
        keep = jnp.logical_or(gt[:, sl], jnp.logical_and(eq[:, sl], pre <= need))
        parts.append(jnp.where(keep, 0.0, NEG_INF))
    return jnp.concatenate(parts, axis=1)


def _softmax_pv(s, v):
    m = jnp.max(s, axis=1, keepdims=True)
    p = jnp.exp2(s - m)
    l = jnp.sum(p, axis=1, keepdims=True)
    return jnp.dot(p.astype(BF16), v, preferred_element_type=F32) / l


def _attn_even_prompt_body(qa_ref, qi_ref, smq_ref, kit_ref, kva_ref, qn_ref, qr_ref, kn_ref, kr_ref,
                           vb_ref, tri_ref, o_ref, key_ref, *, n_top, seq, kstep):
    i = pl.program_id(1)

    def compute(kw):
        rows = i * QB + lax.broadcasted_iota(jnp.int32, (QB, kw), 0)
        cols = lax.broadcasted_iota(jnp.int32, (QB, kw), 1)
        causal = cols <= rows
        kit = kit_ref[:, 0:kw]
        sc = jnp.zeros((QB, kw), F32)
        for h in range(H_I):
            d = jnp.dot(qi_ref[:, h * D_I:(h + 1) * D_I], kit, preferred_element_type=F32)
            sc = sc + smq_ref[:, 64 + h:65 + h] * jnp.maximum(d, 0.0)
        bias = _select_bias(sc, causal, n_top, key_ref.at[:, 0:kw], tri_ref[...])
        rep = H_A // KV_A
        for h in range(H_A):
            g = h // rep
            s = _dot_nt(qa_ref[:, h * HD_A:(h + 1) * HD_A], kva_ref[0:kw, g * HD_A:(g + 1) * HD_A])
            s = s + bias
            o = _softmax_pv(s, kva_ref[0:kw, 128 + g * HD_A:128 + (g + 1) * HD_A])
            o_ref[:, h * HD_A:(h + 1) * HD_A] = o.astype(BF16)
        cbias = jnp.where(causal, 0.0, NEG_INF)
        kr = kr_ref[0:kw, :]
        for h in range(H_B):
            s = (_dot_nt(qn_ref[:, h * NOPE_B:(h + 1) * NOPE_B], kn_ref[0:kw, h * NOPE_B:(h + 1) * NOPE_B])
                 + _dot_nt(qr_ref[:, h * ROPE_B:(h + 1) * ROPE_B], kr))
            s = s + cbias
            o = _softmax_pv(s, vb_ref[0:kw, h * V_B:(h + 1) * V_B])
            o_ref[:, 512 + h * V_B:512 + (h + 1) * V_B] = o.astype(BF16)

    per = kstep // QB
    for v in range(seq // kstep):
        pl.when(i // per == v)(functools.partial(compute, (v + 1) * kstep))


def _key_step(seq):
    return max(QB, seq // 8)


def _attn_even_prompt(qa, qi, sm, kit, kva, qn, qr, kn, kr, vb, tri, nb, seq, n_top):
    nqb = seq // QB
    qrow = lambda w: pl.BlockSpec((QB, w), lambda b, i: (b * nqb + i, 0))
    krow = lambda w: pl.BlockSpec((seq, w), lambda b, i: (b, 0))
    return pl.pallas_call(
        functools.partial(_attn_even_prompt_body, n_top=n_top, seq=seq, kstep=_key_step(seq)),
        grid=(nb, nqb),
        in_specs=[qrow(512), qrow(256), qrow(128),
                  pl.BlockSpec((None, D_I, seq), lambda b, i: (b, 0, 0)),
                  krow(256), qrow(512), qrow(256), krow(512),
                  pl.BlockSpec((None, seq, ROPE_B), lambda b, i: (b, 0, 0)),
                  krow(512),
                  pl.BlockSpec((128, 128), lambda b, i: (0, 0))],
        out_specs=qrow(1024),
        out_shape=jax.ShapeDtypeStruct((nb * seq, 1024), BF16),
        scratch_shapes=[pltpu.VMEM((QB, seq), jnp.int32)],
        compiler_params=_cp(("parallel", "arbitrary")),
        name="attn_even_prompt",
    )(qa, qi, sm, kit, kva, qn, qr, kn, kr, vb, tri)


def _suffix_rest(lk, ms, cw, carry):
    w = lk.shape[1]
    parts = [None] * (w // cw)
    for c in reversed(range(w // cw)):
        lkc = lk[:, c * cw:(c + 1) * cw]
        r = _dot_lsplit2(lkc, ms) + carry
        carry = r[:, 0:1] + lkc[:, 0:1]
        parts[c] = r
    rest = parts[0] if len(parts) == 1 else jnp.concatenate(parts, axis=1)
    return rest, carry


def _sb_prompt_body(q_ref, kv_ref, ms_ref, o_ref, *, seq, cw, kstep):
    i = pl.program_id(1)

    def compute(kw):
        rows = i * QB + lax.broadcasted_iota(jnp.int32, (QB, kw), 0)
        cols = lax.broadcasted_iota(jnp.int32, (QB, kw), 1)
        mask = cols < rows
        ms = ms_ref[...]
        rep = H_D // KV_D
        for h in range(H_D):
            g = h // rep
            z = _dot_nt(q_ref[:, h * HD_D:(h + 1) * HD_D], kv_ref[0:kw, g * HD_D:(g + 1) * HD_D])
            ls = _log_sigmoid(z)
            lk = jnp.where(mask, ls - z, 0.0)
            rest, _ = _suffix_rest(lk, ms, cw, jnp.zeros((QB, 1), F32))
            a = jnp.where(mask, jnp.exp(ls + rest), 0.0)
            o = jnp.dot(a.astype(BF16), kv_ref[0:kw, 256 + g * HD_D:256 + (g + 1) * HD_D],
                        preferred_element_type=F32)
            o_ref[:, h * HD_D:(h + 1) * HD_D] = o.astype(BF16)

    per = kstep // QB
    for v in range(seq // kstep):
        pl.when(i // per == v)(functools.partial(compute, (v + 1) * kstep))


def _sb_prompt(qd, kvd, nb, seq):
    nqb = seq // QB
    kstep = _key_step(seq)
    cw = 256 if kstep % 256 == 0 else 128
    ms = jnp.asarray(np.tril(np.ones((cw, cw), np.float32), -1), BF16)
    return pl.pallas_call(
        functools.partial(_sb_prompt_body, seq=seq, cw=cw, kstep=kstep),
        grid=(nb, nqb),
        in_specs=[pl.BlockSpec((QB, 512), lambda b, i: (b * nqb + i, 0)),
                  pl.BlockSpec((seq, 512), lambda b, i: (b, 0)),
                  pl.BlockSpec((cw, cw), lambda b, i: (0, 0))],
        out_specs=pl.BlockSpec((QB, 512), lambda b, i: (b * nqb + i, 0)),
        out_shape=jax.ShapeDtypeStruct((nb * seq, 512), BF16),
        compiler_params=_cp(("parallel", "arbitrary")),
        name="sb_prompt",
    )(qd, kvd, ms)


def _gdn_body(qkv_ref, sm_ref, gate_ref, cw_ref, al_ref, dtb_ref, alc_ref, dtc_ref, on_ref, bd_ref,
              tril_ref, conv0_ref, s0_ref, o_ref, sout_ref, cout_ref, xbuf, s_scr, *, valid):
    C = CHUNK_C
    c = pl.program_id(1)
    nc = pl.num_programs(1)

    @pl.when(c == 0)
    def _():
        xbuf[0:8, :] = conv0_ref[...]
        s_scr[...] = s0_ref[...]

    xbuf[8:8 + C, :] = qkv_ref[...]
    y = (cw_ref[3:4, :] * xbuf[8:8 + C, :] + cw_ref[2:3, :] * xbuf[7:7 + C, :]
         + cw_ref[1:2, :] * xbuf[6:6 + C, :] + cw_ref[0:1, :] * xbuf[5:5 + C, :])
    xc = _silu(y)

    @pl.when(c == nc - 1)
    def _():
        cout_ref[...] = xbuf[8 + valid - (CONV_W - 1):8 + valid, :]

    xbuf[0:8, :] = xbuf[C:C + 8, :]

    hk = H_C * DK_C
    q, k, v = xc[:, 0:hk], xc[:, hk:2 * hk], xc[:, 2 * hk:]
    bd = bd_ref[...]
    qn = q * lax.rsqrt(_dot_lsplit(q * q, bd) + EPS) * DK_C ** -0.5
    kn = k * lax.rsqrt(_dot_lsplit(k * k, bd) + EPS)

    sm = sm_ref[...]
    smt = sm.T
    g = -jnp.exp(al_ref[...]) * _softplus(sm + dtb_ref[...])
    gt = -jnp.exp(alc_ref[...]) * _softplus(smt[0:H_C, :] + dtc_ref[...])
    beta = jax.nn.sigmoid(sm)
    if valid < C:
        g = jnp.where(lax.broadcasted_iota(jnp.int32, g.shape, 0) < valid, g, 0.0)
        gt = jnp.where(lax.broadcasted_iota(jnp.int32, gt.shape, 1) < valid, gt, 0.0)
        beta = jnp.where(lax.broadcasted_iota(jnp.int32, beta.shape, 0) < valid, beta, 0.0)
    tril = tril_ref[...]
    gc = _dot_rsplit(tril, g)
    gct = _dot_nt_lsplit(gt, tril)

    ri = lax.broadcasted_iota(jnp.int32, (C, C), 0)
    ci = lax.broadcasted_iota(jnp.int32, (C, C), 1)
    incl = ri >= ci
    strict = ri > ci
    levels = max(1, math.ceil(math.log2(valid)))
    hs = range(H_C)
    sls = [slice(h * DK_C, (h + 1) * DK_C) for h in hs]
    gcol = [gc[:, h:h + 1] for h in hs]
    gam = [jnp.exp(jnp.where(incl, gcol[h] - gct[h:h + 1, :], NEG_INF)) for h in hs]
    kh = [kn[:, sls[h]] for h in hs]
    qh = [qn[:, sls[h]] for h in hs]
    bh = [beta[:, H_C + h:H_C + h + 1] for h in hs]
    kb = [kh[h] * bh[h] for h in hs]
    kk = [_dot_nt(jnp.concatenate([kb[h], qh[h]], axis=0), kh[h]) for h in hs]
    pw = [jnp.where(strict, -(kk[h][0:C] * gam[h]), 0.0) for h in hs]
    a_qk = [kk[h][C:2 * C] * gam[h] for h in hs]
    x = [jnp.concatenate([v[:, sls[h]] * bh[h], kb[h] * jnp.exp(gcol[h])], axis=1) for h in hs]
    for lvl in range(levels):
        if lvl < levels - 1:
            y = [_dot(pw[h], jnp.concatenate([x[h], pw[h]], axis=1)) for h in hs]
            x = [x[h] + y[h][:, 0:2 * DV_C] for h in hs]
            pw = [y[h][:, 2 * DV_C:] for h in hs]
        else:
            x = [x[h] + _dot(pw[h], x[h]) for h in hs]
    sh = [s_scr[h] for h in hs]
    ws = [_dot(jnp.concatenate([x[h][:, DV_C:], qh[h] * jnp.exp(gcol[h])], axis=0), sh[h])
          for h in hs]
    v_new = [x[h][:, 0:DV_C] - ws[h][0:C] for h in hs]
    oh = [ws[h][C:2 * C] + _dot(a_qk[h], v_new[h]) for h in hs]
    glast = [gcol[h][C - 1:C, :] for h in hs]
    k_dec = [kh[h] * jnp.exp(glast[h] - gcol[h]) for h in hs]
    s_new = [sh[h] * jnp.exp(glast[h]) + _dot_tn(k_dec[h], v_new[h]) for h in hs]
    for h in hs:
        s_scr[h] = s_new[h]
        on = _rms(oh[h], on_ref[...])
        o_ref[:, sls[h]] = (on * _silu(gate_ref[:, sls[h]])).astype(BF16)

    @pl.when(c == nc - 1)
    def _():
        sout_ref[...] = s_scr[...]


def _dot_nt_lsplit(a, b_exact):
    hi, mid, lo = _split3(a)
    d = lambda x: lax.dot_general(x, b_exact, (((1,), (1,)), ((), ())), preferred_element_type=F32)
    return d(hi) + d(mid) + d(lo)


def _gdn(qkv, sm, gate, conv_w, a_log, dt_bias, o_norm, conv0, s0, ng, nchunk, valid):
    C = CHUNK_C
    pad128 = lambda a: jnp.pad(a.reshape(1, -1), ((0, 0), (0, 128 - a.size)))
    al, dtb = pad128(a_log), pad128(dt_bias)
    alc, dtc = a_log.reshape(H_C, 1), dt_bias.reshape(H_C, 1)
    on = o_norm.reshape(1, DV_C)
    hk = H_C * DK_C
    bd = jnp.asarray(np.kron(np.eye(H_C, dtype=np.float32), np.ones((DK_C, DK_C), np.float32)), BF16)
    tril = jnp.asarray(np.tril(np.ones((C, C), np.float32)), BF16)
    row = lambda w: pl.BlockSpec((C, w), lambda b, c: (b * nchunk + c, 0))
    full = lambda a: pl.BlockSpec(a.shape, lambda b, c: (0,) * a.ndim)
    return pl.pallas_call(
        functools.partial(_gdn_body, valid=valid),
        grid=(ng, nchunk),
        in_specs=[row(3 * hk), row(128), row(hk), full(conv_w), full(al), full(dtb), full(alc), full(dtc),
                  full(on), full(bd), full(tril),
                  pl.BlockSpec((None, 8, 3 * hk), lambda b, c: (b, 0, 0)),
                  pl.BlockSpec((None, H_C, DK_C, DV_C), lambda b, c: (b, 0, 0, 0))],
        out_specs=[row(hk),
                   pl.BlockSpec((None, H_C, DK_C, DV_C), lambda b, c: (b, 0, 0, 0)),
                   pl.BlockSpec((None, CONV_W - 1, 3 * hk), lambda b, c: (b, 0, 0))],
        out_shape=[jax.ShapeDtypeStruct((ng * nchunk * C, hk), BF16),
                   jax.ShapeDtypeStruct((ng, H_C, DK_C, DV_C), F32),
                   jax.ShapeDtypeStruct((ng, CONV_W - 1, 3 * hk), F32)],
        scratch_shapes=[pltpu.VMEM((C + 8, 3 * hk), F32), pltpu.VMEM((H_C, DK_C, DV_C), F32)],
        compiler_params=_cp(("parallel", "arbitrary")),
        name="gdn",
    )(qkv, sm, gate, conv_w, al, dtb, alc, dtc, on, bd, tril, conv0, s0)


def _outproj_body(x_ref, a1_ref, a2_ref, w_ref, o_ref):
    h = a1_ref.shape[1]
    o_ref[...] = (x_ref[...] + jnp.dot(a1_ref[...], w_ref[0:h, :], preferred_element_type=F32)
                  + jnp.dot(a2_ref[...], w_ref[h:, :], preferred_element_type=F32))


def _outproj(x, a1, a2, w, c1, c2):
    n, d = x.shape
    tm = _div_tile(n, 1280)
    return pl.pallas_call(
        _outproj_body,
        grid=(n // tm,),
        in_specs=[pl.BlockSpec((tm, d), lambda i: (i, 0)),
                  pl.BlockSpec((tm, 512), lambda i: (i, c1)),
                  pl.BlockSpec((tm, 512), lambda i: (i, c2)),
                  pl.BlockSpec(w.shape, lambda i: (0, 0))],
        out_specs=pl.BlockSpec((tm, d), lambda i: (i, 0)),
        out_shape=jax.ShapeDtypeStruct((n, d), F32),
        compiler_params=_cp(("parallel",)),
        name="outproj",
    )(x, a1, a2, w)


def _final_norm_body(x_ref, g_ref, o_ref):
    o_ref[...] = _rms(x_ref[...], g_ref[...])


def _final_norm(x, g):
    n, d = x.shape
    tm = _div_tile(n, 1280)
    return pl.pallas_call(
        _final_norm_body,
        grid=(n // tm,),
        in_specs=[pl.BlockSpec((tm, d), lambda i: (i, 0)), pl.BlockSpec((1, d), lambda i: (0, 0))],
        out_specs=pl.BlockSpec((tm, d), lambda i: (i, 0)),
        out_shape=jax.ShapeDtypeStruct((n, d), F32),
        compiler_params=_cp(("parallel",)),
        name="final_norm",
    )(x, g)


def _page_specs(shape_tail, layer, n_pages, pp, nsteps, reverse=False):
    specs = []
    for p in range(pp):
        def imap(b, st, pt, p=p):
            if reverse:
                lp = n_pages - 1 - (jnp.maximum(st - 1, 0) * pp + p)
            else:
                lp = jnp.minimum(st, nsteps - 1) * pp + p
            return (layer, pt[b * n_pages + lp]) + (0,) * len(shape_tail)
        specs.append(pl.BlockSpec((None, None) + shape_tail, imap))
    return specs


def _idx_sample_body(pt_ref, qi_ref, wi_ref, knew_ref, *rest, pp):
    pages = rest[:pp]
    sp_ref, sn_ref = rest[pp], rest[pp + 1]
    st = pl.program_id(1)
    ns = pl.num_programs(1)
    qi = qi_ref[...]
    wi = wi_ref[...]

    def score(kt):
        d = _dot(qi, kt)
        acc = jnp.zeros((TP, PAGE), F32)
        for h in range(H_I):
            acc = acc + wi[:, h:h + 1] * jnp.maximum(d[h * TP:(h + 1) * TP, :], 0.0)
        return acc

    @pl.when(st < ns - 1)
    def _():
        for p in range(pp):
            sp_ref[:, p * PAGE:(p + 1) * PAGE] = score(pages[p][...])

    @pl.when(st == ns - 1)
    def _():
        sn_ref[...] = score(knew_ref[...])


def _idx_sample(pt, qi_s, wi_s, ki_new, pool_i, layer, ndb, n_pages, pp):
    nsteps = n_pages // pp
    grid_spec = pltpu.PrefetchScalarGridSpec(
        num_scalar_prefetch=1,
        grid=(ndb, nsteps + 1),
        in_specs=[pl.BlockSpec((None, H_I * TP, D_I), lambda b, st, pt: (b, 0, 0)),
                  pl.BlockSpec((None, TP, H_I), lambda b, st, pt: (b, 0, 0)),
                  pl.BlockSpec((None, D_I, PAGE), lambda b, st, pt: (b, 0, 0))]
        + _page_specs((D_I, PAGE), layer, n_pages, pp, nsteps),
        out_specs=[pl.BlockSpec((None, TP, pp * PAGE), lambda b, st, pt: (b, 0, jnp.minimum(st, nsteps - 1))),
                   pl.BlockSpec((None, TP, PAGE), lambda b, st, pt: (b, 0, 0))],
    )
    return pl.pallas_call(
        functools.partial(_idx_sample_body, pp=pp),
        grid_spec=grid_spec,
        out_shape=[jax.ShapeDtypeStruct((ndb, TP, n_pages * PAGE), F32),
                   jax.ShapeDtypeStruct((ndb, TP, PAGE), F32)],
        compiler_params=_cp(("parallel", "arbitrary")),
        name="idx_sample",
    )(pt, qi_s, wi_s, ki_new, *([pool_i] * pp))


def _select_sample_body(sp_ref, sn_ref, tri_ref, bp_ref, bn_ref, key_ref, *, n_top, past, t_new):
    sc = jnp.concatenate([sp_ref[...], sn_ref[...]], axis=1)
    r, w = sc.shape
    t = lax.broadcasted_iota(jnp.int32, (r, w), 0) & (TP - 1)
    cols = lax.broadcasted_iota(jnp.int32, (r, w), 1)
    valid = jnp.logical_or(cols < past, jnp.logical_and(cols - past <= t, cols - past < t_new))
    bias = _select_bias(sc, valid, n_top, key_ref, tri_ref[...])
    bp_ref[...] = bias[:, 0:past]
    bn_ref[...] = bias[:, past:]


def _select_sample(sp, sn, tri, n_top, t_new):
    rows, past = sp.shape
    tr = _div_tile(rows, 64)
    return pl.pallas_call(
        functools.partial(_select_sample_body, n_top=n_top, past=past, t_new=t_new),
        grid=(rows // tr,),
        in_specs=[pl.BlockSpec((tr, past), lambda i: (i, 0)),
                  pl.BlockSpec((tr, PAGE), lambda i: (i, 0)),
                  pl.BlockSpec((128, 128), lambda i: (0, 0))],
        out_specs=[pl.BlockSpec((tr, past), lambda i: (i, 0)),
                   pl.BlockSpec((tr, PAGE), lambda i: (i, 0))],
        out_shape=[jax.ShapeDtypeStruct((rows, past), F32), jax.ShapeDtypeStruct((rows, PAGE), F32)],
        scratch_shapes=[pltpu.VMEM((tr, past + PAGE), jnp.int32)],
        compiler_params=_cp(("parallel",)),
        name="select_sample",
    )(sp, sn, tri)


def _online_update(m_ref, l_ref, acc_ref, rows, s, pv):
    m_old = m_ref[rows, :]
    m_new = jnp.maximum(m_old, jnp.max(s, axis=1, keepdims=True))
    m_safe = jnp.where(m_new == NEG_INF, 0.0, m_new)
    p = jnp.exp2(s - m_safe)
    alpha = jnp.exp2(m_old - m_safe)
    l_ref[rows, :] = alpha * l_ref[rows, :] + jnp.sum(p, axis=1, keepdims=True)
    pb = p.astype(BF16)
    acc = alpha * acc_ref[rows, :]
    for i in range(s.shape[1] // PAGE):
        acc = acc + pv(pb[:, i * PAGE:(i + 1) * PAGE], i)
    acc_ref[rows, :] = acc
    m_ref[rows, :] = m_new


def _lane_cat(parts):
    return parts[0] if len(parts) == 1 else jnp.concatenate(parts, axis=1)


def _attn_even_sample_body(pt_ref, qa_ref, qn_ref, qr_ref, bp_ref, bn_ref, ktn_ref, vtn_ref, cn_ref, krn_ref,
                           wk_ref, wv_ref, *rest, pp, t_new):
    kpg, vpg, cpg, rpg = rest[0:pp], rest[pp:2 * pp], rest[2 * pp:3 * pp], rest[3 * pp:4 * pp]
    o_ref = rest[4 * pp]
    qlat, m_a, l_a, acc_a, m_b, l_b, acc_b = rest[4 * pp + 1:]
    st = pl.program_id(1)
    ns = pl.num_programs(1)
    rep = H_A // KV_A
    rows_g = rep * TP

    @pl.when(st == 0)
    def _():
        for h in range(H_B):
            qlat[h * TP:(h + 1) * TP, :] = _dot_nt(qn_ref[h * TP:(h + 1) * TP, :], wk_ref[h]).astype(BF16)
        m_a[...] = jnp.full_like(m_a, NEG_INF)
        m_b[...] = jnp.full_like(m_b, NEG_INF)
        l_a[...] = jnp.zeros_like(l_a)
        l_b[...] = jnp.zeros_like(l_b)
        acc_a[...] = jnp.zeros_like(acc_a)
        acc_b[...] = jnp.zeros_like(acc_b)

    def update(kt, vt, c, krt, bias8, mla_bias):
        n = len(kt)
        bias = jnp.concatenate([bias8] * rep, axis=0)
        for g in range(KV_A):
            rows = slice(g * rows_g, (g + 1) * rows_g)
            qg = qa_ref[rows, :]
            s = _lane_cat([_dot(qg, kt[i][g]) for i in range(n)]) + bias
            _online_update(m_a, l_a, acc_a, rows, s, lambda pt_, i, g=g: _dot_nt(pt_, vt[i][g]))
        ql, qr = qlat[...], qr_ref[...]
        cb = [c[i][...].astype(BF16) for i in range(n)]
        s = _lane_cat([_dot_nt(ql, cb[i]) + _dot(qr, krt[i][...]) for i in range(n)])
        if mla_bias is not None:
            s = s + mla_bias
        _online_update(m_b, l_b, acc_b, slice(0, H_B * TP), s,
                       lambda pt_, i: jnp.dot(pt_, cb[i], preferred_element_type=F32))

    @pl.when(st < ns - 1)
    def _():
        update(kpg, vpg, cpg, rpg, bp_ref[...], None)

    @pl.when(st == ns - 1)
    def _():
        t = lax.broadcasted_iota(jnp.int32, (H_B * TP, PAGE), 0) & (TP - 1)
        j = lax.broadcasted_iota(jnp.int32, (H_B * TP, PAGE), 1)
        ok = jnp.logical_and(j <= t, j < t_new)
        update([ktn_ref], [vtn_ref], [cn_ref], [krn_ref], bn_ref[...], jnp.where(ok, 0.0, NEG_INF))
        oa = acc_a[...] / l_a[...]
        ol = (acc_b[...] / l_b[...]).astype(BF16)
        for h in range(H_A):
            o_ref[:, h * HD_A:(h + 1) * HD_A] = oa[h * TP:(h + 1) * TP, :].astype(BF16)
        for h in range(H_B):
            ob = jnp.dot(ol[h * TP:(h + 1) * TP, :], wv_ref[h], preferred_element_type=F32)
            o_ref[:, 512 + h * V_B:512 + (h + 1) * V_B] = ob.astype(BF16)


def _attn_even_sample(pt, qa_s, qn_s, qr_s, bias_p, bias_n, kt_new, vt_new, c_new, krt_new, wk, wv,
                      pool_kt, pool_vt, pool_c, pool_rt, layer, ndb, n_pages, pp, t_new):
    nsteps = n_pages // pp
    seqn = lambda a: pl.BlockSpec((None,) + a.shape[1:], lambda b, st, pt: (b,) + (0,) * (a.ndim - 1))
    full3 = lambda a: pl.BlockSpec(a.shape, lambda b, st, pt: (0, 0, 0))
    grid_spec = pltpu.PrefetchScalarGridSpec(
        num_scalar_prefetch=1,
        grid=(ndb, nsteps + 1),
        in_specs=[seqn(qa_s), seqn(qn_s), seqn(qr_s),
                  pl.BlockSpec((None, TP, pp * PAGE), lambda b, st, pt: (b, 0, jnp.minimum(st, nsteps - 1))),
                  seqn(bias_n), seqn(kt_new), seqn(vt_new), seqn(c_new), seqn(krt_new), full3(wk), full3(wv)]
        + _page_specs((KV_A, HD_A, PAGE), layer, n_pages, pp, nsteps)
        + _page_specs((KV_A, HD_A, PAGE), layer, n_pages, pp, nsteps)
        + _page_specs((PAGE, KV_LORA), layer, n_pages, pp, nsteps)
        + _page_specs((ROPE_B, PAGE), layer, n_pages, pp, nsteps),
        out_specs=pl.BlockSpec((None, TP, 1024), lambda b, st, pt: (b, 0, 0)),
        scratch_shapes=[pltpu.VMEM((H_B * TP, KV_LORA), BF16),
                        pltpu.VMEM((H_A * TP, 1), F32), pltpu.VMEM((H_A * TP, 1), F32),
                        pltpu.VMEM((H_A * TP, HD_A), F32),
                        pltpu.VMEM((H_B * TP, 1), F32), pltpu.VMEM((H_B * TP, 1), F32),
                        pltpu.VMEM((H_B * TP, KV_LORA), F32)],
    )
    return pl.pallas_call(
        functools.partial(_attn_even_sample_body, pp=pp, t_new=t_new),
        grid_spec=grid_spec,
        out_shape=jax.ShapeDtypeStruct((ndb, TP, 1024), BF16),
        compiler_params=_cp(("parallel", "arbitrary")),
        name="attn_even_sample",
    )(pt, qa_s, qn_s, qr_s, bias_p, bias_n, kt_new, vt_new, c_new, krt_new, wk, wv,
      *([pool_kt] * pp), *([pool_vt] * pp), *([pool_c] * pp), *([pool_rt] * pp))


def _sb_sample_body(pt_ref, q_ref, ktn_ref, vtn_ref, ms_ref, *rest, pp, t_new):
    kpg, vpg = rest[0:pp], rest[pp:2 * pp]
    o_ref = rest[2 * pp]
    carry, acc = rest[2 * pp + 1:]
    st = pl.program_id(1)
    ns = pl.num_programs(1)
    rep = H_D // KV_D
    rows_g = rep * TP
    nq = H_D * TP
    ms = ms_ref[...]

    def update(kt, vt, mask):
        n = len(kt)
        z = jnp.concatenate(
            [_dot(q_ref[g * rows_g:(g + 1) * rows_g, :], kt[i][g]) for i in range(n) for g in range(KV_D)],
            axis=0)
        ls = _log_sigmoid(z)
        lk = ls - z
        if mask is not None:
            lk = jnp.where(mask, lk, 0.0)
        within = _dot_lsplit2(lk, ms)
        tot = within[:, 0:1] + lk[:, 0:1]
        c = carry[...]
        rest_ = []
        for i in range(n):
            rest_.append(within[i * nq:(i + 1) * nq, :] + c)
            c = c + tot[i * nq:(i + 1) * nq, :]
        carry[...] = c
        a = jnp.exp(ls + (rest_[0] if n == 1 else jnp.concatenate(rest_, axis=0)))
        if mask is not None:
            a = jnp.where(mask, a, 0.0)
        ab = a.astype(BF16)
        for g in range(KV_D):
            o = acc[g * rows_g:(g + 1) * rows_g, :]
            for i in range(n):
                o = o + _dot_nt(ab[i * nq + g * rows_g:i * nq + (g + 1) * rows_g, :], vt[i][g])
            acc[g * rows_g:(g + 1) * rows_g, :] = o

    @pl.when(st == 0)
    def _():
        carry[...] = jnp.zeros_like(carry)
        acc[...] = jnp.zeros_like(acc)
        t = lax.broadcasted_iota(jnp.int32, (nq, PAGE), 0) & (TP - 1)
        j = lax.broadcasted_iota(jnp.int32, (nq, PAGE), 1)
        update([ktn_ref], [vtn_ref], jnp.logical_and(j < t, j < t_new))

    @pl.when(st > 0)
    def _():
        update(kpg, vpg, None)

    @pl.when(st == ns - 1)
    def _():
        a = acc[...]
        for h in range(H_D):
            o_ref[:, h * HD_D:(h + 1) * HD_D] = a[h * TP:(h + 1) * TP, :].astype(BF16)


def _sb_sample(pt, qd_s, kt_new, vt_new, pool_kt, pool_vt, layer, ndb, n_pages, pp, t_new):
    nsteps = n_pages // pp
    ms = jnp.asarray(np.tril(np.ones((PAGE, PAGE), np.float32), -1), BF16)
    seqn = lambda a: pl.BlockSpec((None,) + a.shape[1:], lambda b, st, pt: (b,) + (0,) * (a.ndim - 1))
    grid_spec = pltpu.PrefetchScalarGridSpec(
        num_scalar_prefetch=1,
        grid=(ndb, nsteps + 1),
        in_specs=[seqn(qd_s), seqn(kt_new), seqn(vt_new), pl.BlockSpec((PAGE, PAGE), lambda b, st, pt: (0, 0))]
        + _page_specs((KV_D, HD_D, PAGE), layer, n_pages, pp, nsteps, reverse=True)
        + _page_specs((KV_D, HD_D, PAGE), layer, n_pages, pp, nsteps, reverse=True),
        out_specs=pl.BlockSpec((None, TP, 512), lambda b, st, pt: (b, 0, 0)),
        scratch_shapes=[pltpu.VMEM((H_D * TP, 1), F32), pltpu.VMEM((H_D * TP, HD_D), F32)],
    )
    return pl.pallas_call(
        functools.partial(_sb_sample_body, pp=pp, t_new=t_new),
        grid_spec=grid_spec,
        out_shape=jax.ShapeDtypeStruct((ndb, TP, 512), BF16),
        compiler_params=_cp(("parallel", "arbitrary")),
        name="sb_sample",
    )(pt, qd_s, kt_new, vt_new, ms, *([pool_kt] * pp), *([pool_vt] * pp))


def _rows_ht(a, ndb, t, heads, d):
    a = a.reshape(ndb, t, heads, d).transpose(0, 2, 1, 3)
    a = jnp.pad(a, ((0, 0), (0, 0), (0, TP - t), (0, 0)))
    return a.reshape(ndb, heads * TP, d)


def _pad_rows(a, ndb, t, rows):
    a = a.reshape(ndb, t, a.shape[-1])
    return jnp.pad(a, ((0, 0), (0, rows - t), (0, 0)))


def _new_t(a, ndb, t, heads, d):
    a = a.reshape(ndb, t, heads, d).transpose(0, 2, 3, 1)
    return jnp.pad(a, ((0, 0), (0, 0), (0, 0), (0, PAGE - t))).astype(BF16)


def _unpad_o(o, t):
    return o[:, :t, :].reshape(-1, o.shape[-1])


def kernel(x_prompt, x_sample, cache_a_k, cache_a_v, cache_a_idx, cache_b_ckv, cache_b_krope, cache_d_k,
           cache_d_v, state_c_ssm, state_c_conv, page_table, norm_ffn, w_ffn_gate, w_ffn_up, w_ffn_down,
           norm_mix, w_in_even, w_out_even, mla_q_norm, mla_w_qb, mla_kv_norm, mla_w_kvb, w_in_odd,
           w_out_odd, gdn_conv_w, gdn_a_log, gdn_dt_bias, gdn_o_norm, norm_final):
    nb, seq, d = x_prompt.shape
    ndb, t_new, _ = x_sample.shape
    depth = norm_mix.shape[0]
    n_pages = page_table.shape[1]
    past = n_pages * PAGE
    n_pool = cache_a_k.shape[1]
    np_, ns_ = nb * seq, ndb * t_new
    assert seq % QB == 0 and seq % CHUNK_C == 0 and t_new <= TP and t_new >= CONV_W - 1
    n_top_p = min(TOPK_MAX, seq // 4)
    n_top_s = min(TOPK_MAX, (past + t_new) // 4)
    pp = max(p for p in (16, 8, 4, 2, 1) if n_pages % p == 0)

    tm_p = _div_tile(math.gcd(seq, ns_), 512)
    n_ptiles = np_ // tm_p
    tiles_per_seq = seq // tm_p
    tab_index = lambda i: jnp.where(i < n_ptiles, i % tiles_per_seq, tiles_per_seq)

    pos = jnp.concatenate([jnp.arange(seq), past + (jnp.arange(tm_p) % t_new)])
    tabs = (_rope_table(pos, HD_A, ROT_A), _rope_table(pos, D_I, ROT_I), _rope_table(pos, ROPE_B, ROPE_B),
            _small_table(pos))

    wg_all, wu_all, wd_all = (w.astype(BF16) for w in (w_ffn_gate, w_ffn_up, w_ffn_down))
    g_ffn = norm_ffn.reshape(depth * 2, 1, d)
    we = w_in_even
    w_even = jnp.concatenate([we[..., 0:1024], we[..., 1064:1320], we[..., 1320:1448], we[..., 1024:1056],
                              we[..., 1448:1480], we[..., 1056:1064],
                              jnp.zeros(we.shape[:2] + (56,), we.dtype)], axis=-1).astype(BF16)
    wo = w_in_odd
    w_odd = jnp.concatenate([wo[..., 0:1536], wo[..., 1552:3088], wo[..., 1536:1552],
                             jnp.zeros(wo.shape[:2] + (112,), wo.dtype)], axis=-1).astype(BF16)
    ne = w_in_even.shape[0]
    wqb = mla_w_qb.reshape(ne, Q_LORA, H_B, NOPE_B + ROPE_B)
    wqb = jnp.concatenate([wqb[..., :NOPE_B].reshape(ne, Q_LORA, -1),
                           wqb[..., NOPE_B:].reshape(ne, Q_LORA, -1)], axis=-1).astype(BF16)
    wkvb4 = mla_w_kvb.reshape(ne, KV_LORA, H_B, NOPE_B + V_B)
    wkvb = jnp.concatenate([wkvb4[..., :NOPE_B].reshape(ne, KV_LORA, -1),
                            wkvb4[..., NOPE_B:].reshape(ne, KV_LORA, -1)], axis=-1).astype(BF16)
    wk_heads = wkvb4[..., :NOPE_B].transpose(0, 2, 1, 3).astype(BF16)
    wv_heads = wkvb4[..., NOPE_B:].transpose(0, 2, 1, 3).astype(BF16)
    w_out_e = w_out_even.astype(BF16)
    w_out_o = w_out_odd.astype(BF16)

    tri = jnp.asarray(np.triu(np.ones((128, 128), np.float32)), BF16)
    pt_flat = page_table.reshape(-1).astype(jnp.int32)
    pool_akt = cache_a_k.transpose(0, 1, 3, 4, 2)
    pool_avt = cache_a_v.transpose(0, 1, 3, 4, 2)
    pool_ait = cache_a_idx.transpose(0, 1, 3, 2)
    pool_brt = cache_b_krope.transpose(0, 1, 3, 2)
    pool_dkt = cache_d_k.transpose(0, 1, 3, 4, 2)
    pool_dvt = cache_d_v.transpose(0, 1, 3, 4, 2)

    x = jnp.concatenate([x_prompt.reshape(np_, d), x_sample.reshape(ns_, d)], axis=0)
    outs = {k: [] for k in ("a_k", "a_v", "a_idx", "b_ckv", "b_krope", "d_k", "d_v", "c_ssm", "c_conv")}

    for l in range(depth):
        j = l // 2
        x = _ffn(x, g_ffn, wg_all, wu_all, wd_all, l, 0)
        g_mix = norm_mix[l].reshape(1, d)
        if l % 2 == 0:
            (qa, ka, va, kva, qi, sm, qn, qr, c, kn, vb) = _proj_even(
                x, g_mix, w_even[j], tabs, mla_q_norm[j].reshape(1, -1), wqb[j],
                mla_kv_norm[j].reshape(1, -1), wkvb[j], tm_p, tab_index)
            outs["a_k"].append(ka)
            outs["a_v"].append(va)
            outs["a_idx"].append(sm[:, 0:D_I])
            outs["b_ckv"].append(c)
            outs["b_krope"].append(sm[:, D_I:D_I + ROPE_B])
            smp = sm[:np_].reshape(nb, seq, 128)
            kit = smp[:, :, 0:D_I].transpose(0, 2, 1).astype(BF16)
            krp = smp[:, :, D_I:D_I + ROPE_B].astype(BF16)
            o_p = _attn_even_prompt(qa, qi, sm, kit, kva, qn, qr, kn, krp, vb, tri, nb, seq, n_top_p)
            sms = sm[np_:]
            qi_s = _rows_ht(qi[np_:], ndb, t_new, H_I, D_I)
            wi_s = _pad_rows(sms[:, 64:64 + H_I], ndb, t_new, TP)
            ki_new = _new_t(sms[:, 0:D_I], ndb, t_new, 1, D_I).reshape(ndb, D_I, PAGE)
            sp, sn = _idx_sample(pt_flat, qi_s, wi_s, ki_new, pool_ait, j, ndb, n_pages, pp)
            bp, bn = _select_sample(sp.reshape(ndb * TP, past), sn.reshape(ndb * TP, PAGE), tri, n_top_s, t_new)
            o_s = _attn_even_sample(
                pt_flat, _rows_ht(qa[np_:], ndb, t_new, H_A, HD_A), _rows_ht(qn[np_:], ndb, t_new, H_B, NOPE_B),
                _rows_ht(qr[np_:], ndb, t_new, H_B, ROPE_B), bp.reshape(ndb, TP, past), bn.reshape(ndb, TP, PAGE),
                _new_t(ka[np_:], ndb, t_new, KV_A, HD_A), _new_t(va[np_:], ndb, t_new, KV_A, HD_A),
                _pad_rows(c[np_:], ndb, t_new, PAGE).astype(BF16),
                _new_t(sms[:, D_I:D_I + ROPE_B], ndb, t_new, 1, ROPE_B).reshape(ndb, ROPE_B, PAGE),
                wk_heads[j], wv_heads[j], pool_akt, pool_avt, cache_b_ckv, pool_brt, j, ndb, n_pages, pp, t_new)
            o = jnp.concatenate([o_p, _unpad_o(o_s, t_new)], axis=0)
            x = _outproj(x, o, o, w_out_e[j], 0, 1)
        else:
            qkv, gate, qd, kd, vd, kvd, sm = _proj_odd(x, g_mix, w_odd[j], tm_p)
            outs["d_k"].append(kd)
            outs["d_v"].append(vd)
            args = (gdn_conv_w[j], gdn_a_log[j], gdn_dt_bias[j], gdn_o_norm[j])
            zc = jnp.zeros((nb, 8, C_CONV_CH), F32)
            zs = jnp.zeros((nb, H_C, DK_C, DV_C), F32)
            oc_p, s_p, cv_p = _gdn(qkv, sm, gate, *args, zc, zs, nb, seq // CHUNK_C, CHUNK_C)
            od_p = _sb_prompt(qd, kvd, nb, seq)
            padc = lambda a: _pad_rows(a[np_:], ndb, t_new, CHUNK_C).reshape(ndb * CHUNK_C, -1)
            conv0 = jnp.pad(state_c_conv[j], ((0, 0), (8 - (CONV_W - 1), 0), (0, 0)))
            oc_s, s_s, cv_s = _gdn(padc(qkv), padc(sm), padc(gate), *args, conv0, state_c_ssm[j], ndb, 1, t_new)
            oc_s = oc_s.reshape(ndb, CHUNK_C, -1)[:, :t_new].reshape(ns_, -1)
            od_s = _sb_sample(pt_flat, _rows_ht(qd[np_:], ndb, t_new, H_D, HD_D),
                              _new_t(kd[np_:], ndb, t_new, KV_D, HD_D), _new_t(vd[np_:], ndb, t_new, KV_D, HD_D),
                              pool_dkt, pool_dvt, j, ndb, n_pages, pp, t_new)
            outs["c_ssm"].append((s_p, s_s))
            outs["c_conv"].append((cv_p, cv_s))
            oc = jnp.concatenate([oc_p, oc_s], axis=0)
            od = jnp.concatenate([od_p, _unpad_o(od_s, t_new)], axis=0)
            x = _outproj(x, oc, od, w_out_o[j], 0, 0)
        x = _ffn(x, g_ffn, wg_all, wu_all, wd_all, l, 1)

    y = _final_norm(x, norm_final.reshape(1, d))

    def group(name, lo, hi, lead, tail):
        return jnp.stack([a[lo:hi].reshape(lead + tail) for a in outs[name]])

    res = [y[:np_].reshape(nb, seq, d), y[np_:].reshape(ndb, t_new, d)]
    for lo, hi, lead, gi in ((0, np_, (nb, seq), 0), (np_, np_ + ns_, (ndb, t_new), 1)):
        res += [group("a_k", lo, hi, lead, (KV_A, HD_A)), group("a_v", lo, hi, lead, (KV_A, HD_A)),
                group("a_idx", lo, hi, lead, (D_I,)), group("b_ckv", lo, hi, lead, (KV_LORA,)),
                group("b_krope", lo, hi, lead, (ROPE_B,)), group("d_k", lo, hi, lead, (KV_D, HD_D)),
                group("d_v", lo, hi, lead, (KV_D, HD_D)),
                jnp.stack([a[gi] for a in outs["c_ssm"]]), jnp.stack([a[gi] for a in outs["c_conv"]])]
    return tuple(res)
```
